```python
import jax, jax.numpy as jnp
from jax import lax
import numpy as np

D_MODEL = 1024
BATCH = 8
SEQ = 8192
DEPTH = 1
DEC_BATCH = 4
DEC_SEQ = 4096
PAST_LEN = 128

N_HEADS_M = 4
DH_M = 256
W_M = N_HEADS_M * DH_M
CHUNK = 64
N_HEADS_A = 16
N_KV_A = 4
GROUP_A = N_HEADS_A // N_KV_A
DH_A = 64
W_A = N_HEADS_A * DH_A
Q_BLOCK = 128
GRID_W = 64
AXIS_DIM = DH_A // 2
ROPE_THETA = 10000.0
EPS = 1e-6

SPLIT_SIZES = (W_M, W_M, W_M, W_M, W_M, 4 * N_HEADS_M,
               W_A, N_KV_A * DH_A, N_KV_A * DH_A, W_A, D_MODEL, D_MODEL)
N_IN_COLS = sum(SPLIT_SIZES)

kernel_name = "hybrid_mlstm_gqa_axial_encoder"


def _rmsnorm(x, g):
    xf = x.astype(jnp.float32)
    y = xf * lax.rsqrt(jnp.mean(xf * xf, axis=-1, keepdims=True) + EPS) * g.astype(jnp.float32)
    return y.astype(x.dtype)


def _mlstm_dir(q, k, v, ig, lf):
    B, H, S, Dh = q.shape
    NC = S // CHUNK

    def chunks(t):
        return jnp.moveaxis(t.reshape(B, H, NC, CHUNK, *t.shape[3:]), 2, 0)

    lower = jnp.tril(jnp.ones((CHUNK, CHUNK), dtype=bool))

    def step(carry, xs):
        C, n, m = carry
        qc, kc, vc, ic, fc = xs
        b = jnp.cumsum(fc, axis=-1)
        dlog = jnp.where(lower, b[..., :, None] - b[..., None, :] + ic[..., None, :], -jnp.inf)
        g = b + m[..., None]
        mt = jnp.maximum(g, jnp.max(dlog, axis=-1))
        s = jnp.einsum('bhtd,bhsd->bhts', qc, kc) * jnp.exp(dlog - mt[..., None])
        inter = jnp.exp(g - mt)
        num = jnp.einsum('bhts,bhsd->bhtd', s, vc) + inter[..., None] * jnp.einsum('bhek,bhtk->bhte', C, qc)
        den = jnp.sum(s, axis=-1) + inter * jnp.einsum('bhk,bhtk->bht', n, qc)
        h = num / jnp.maximum(jnp.abs(den), jnp.exp(-mt))[..., None]
        bL = b[..., -1]
        wl = bL[..., None] - b + ic
        m_new = jnp.maximum(bL + m, jnp.max(wl, axis=-1))
        decay = jnp.exp(bL + m - m_new)
        ws = jnp.exp(wl - m_new[..., None])
        C_new = decay[..., None, None] * C + jnp.einsum('bhse,bhsk->bhek', vc * ws[..., None], kc)
        n_new = decay[..., None] * n + jnp.einsum('bhs,bhsk->bhk', ws, kc)
        return (C_new, n_new, m_new), h

    init = (jnp.zeros((B, H, Dh, Dh), jnp.float32), jnp.zeros((B, H, Dh), jnp.float32),
            jnp.zeros((B, H), jnp.float32))
    _, hs = lax.scan(step, init, (chunks(q), chunks(k), chunks(v), chunks(ig), chunks(lf)))
    return jnp.moveaxis(hs, 0, 2).reshape(B, H, S, Dh)


def _rope_axis(seg, pos):
    half = AXIS_DIM // 2
    freqs = ROPE_THETA ** (-jnp.arange(0, AXIS_DIM, 2, dtype=jnp.float32) / AXIS_DIM)
    ang = pos.astype(jnp.float32)[:, None] * freqs[None, :]
    cos = jnp.cos(ang)[None, :, None, :]
    sin = jnp.sin(ang)[None, :, None, :]
    x1, x2 = seg[..., :half], seg[..., half:]
    return jnp.concatenate([x1 * cos - x2 * sin, x2 * cos + x1 * sin], axis=-1)


def _rope_2d(x, rows, cols):
    return jnp.concatenate([_rope_axis(x[..., :AXIS_DIM], rows), _rope_axis(x[..., AXIS_DIM:], cols)], axis=-1)


def _head_rms(x, g):
    return x * lax.rsqrt(jnp.mean(x * x, axis=-1, keepdims=True) + EPS) * g.astype(jnp.float32)


def _encoder_layer(x, w_in, b_if, norm_pre, q_gain, k_gain, w_mo, w_ao, w_out):
    B, S, _ = x.shape
    f32 = jnp.float32
    h = _rmsnorm(x, norm_pre)
    p = h @ w_in
    idx = [int(i) for i in np.cumsum(SPLIT_SIZES)[:-1]]
    q_m, k_m, v_m, o_m, z_m, if_raw, q_a, k_a, v_a, z_a, g_m, g_a = jnp.split(p, idx, axis=-1)

    if_pre = if_raw.astype(f32).reshape(B, S, 2, 2, N_HEADS_M) + b_if.astype(f32)
    ig = jnp.transpose(if_pre[:, :, :, 0], (2, 0, 3, 1))
    lf = jnp.transpose(jax.nn.log_sigmoid(if_pre[:, :, :, 1]), (2, 0, 3, 1))

    def to_heads(t):
        return t.astype(f32).reshape(B, S, N_HEADS_M, DH_M).transpose(0, 2, 1, 3)

    qm, km, vm = to_heads(q_m), to_heads(k_m) * (DH_M ** -0.5), to_heads(v_m)
    h_f = _mlstm_dir(qm, km, vm, ig[0], lf[0])
    flip = lambda t: jnp.flip(t, axis=2)
    h_b = flip(_mlstm_dir(flip(qm), flip(km), flip(vm), flip(ig[1]), flip(lf[1])))
    h_mem = (h_f + h_b).transpose(0, 2, 1, 3).reshape(B, S, W_M)
    y_m = (h_mem * jax.nn.sigmoid(o_m.astype(f32)) * jax.nn.silu(z_m.astype(f32))).astype(x.dtype) @ w_mo

    ROWS = S // GRID_W
    rows = jnp.broadcast_to(jnp.arange(ROWS)[:, None], (ROWS, GRID_W)).reshape(-1)
    cols = jnp.broadcast_to(jnp.arange(GRID_W)[None, :], (ROWS, GRID_W)).reshape(-1)
    qa = _rope_2d(_head_rms(q_a.astype(f32).reshape(B, S, N_HEADS_A, DH_A), q_gain), rows, cols)
    ka = _rope_2d(_head_rms(k_a.astype(f32).reshape(B, S, N_KV_A, DH_A), k_gain), rows, cols)
    va = v_a.astype(f32).reshape(B, S, N_KV_A, DH_A)
    nqb = S // Q_BLOCK
    qblocks = jnp.moveaxis(qa.reshape(B, nqb, Q_BLOCK, N_KV_A, GROUP_A, DH_A), 1, 0)
    scale = DH_A ** -0.5

    def attend(qb):
        sc = jnp.einsum('bqkgd,bskd->bkgqs', qb, ka) * scale
        pr = jax.nn.softmax(sc, axis=-1)
        return jnp.einsum('bkgqs,bskd->bqkgd', pr, va)

    att = jnp.moveaxis(lax.map(attend, qblocks), 0, 1).reshape(B, S, W_A)
    y_a = (att * jax.nn.silu(z_a.astype(f32))).astype(x.dtype) @ w_ao

    merged = jax.nn.sigmoid(g_m) * y_m + jax.nn.sigmoid(g_a) * y_a
    return x + merged @ w_out


def setup_inputs(seed: int = 0) -> dict:
    key = jax.random.key(seed)
    ks = jax.random.split(key, 14)
    nrm = jax.random.normal
    x_prompt = nrm(ks[0], (BATCH, SEQ, D_MODEL), jnp.float32)
    x_sample = nrm(ks[1], (DEC_BATCH, DEC_SEQ, D_MODEL), jnp.float32)
    w_in = nrm(ks[2], (DEPTH, D_MODEL, N_IN_COLS), jnp.float32) * D_MODEL ** -0.5
    ib = 0.1 * nrm(ks[3], (DEPTH, 2, 1, N_HEADS_M), jnp.float32)
    fb = jnp.linspace(3.0, 6.0, N_HEADS_M, dtype=jnp.float32) + 0.1 * nrm(ks[4], (DEPTH, 2, 1, N_HEADS_M), jnp.float32)
    b_if = jnp.concatenate([ib, fb], axis=2)
    norm_pre = 1.0 + 0.05 * nrm(ks[5], (DEPTH, D_MODEL), jnp.float32)
    q_gain = 1.0 + 0.05 * nrm(ks[6], (DEPTH, DH_A), jnp.float32)
    k_gain = 1.0 + 0.05 * nrm(ks[7], (DEPTH, DH_A), jnp.float32)
    w_mo = nrm(ks[8], (DEPTH, W_M, D_MODEL), jnp.float32) * W_M ** -0.5
    w_ao = nrm(ks[9], (DEPTH, W_A, D_MODEL), jnp.float32) * W_A ** -0.5
    w_out = nrm(ks[10], (DEPTH, D_MODEL, D_MODEL), jnp.float32) * D_MODEL ** -0.5
    norm_final = 1.0 + 0.05 * nrm(ks[11], (D_MODEL,), jnp.float32)
    return {"x_prompt": x_prompt, "x_sample": x_sample, "w_in": w_in, "b_if": b_if,
            "norm_pre": norm_pre, "q_gain": q_gain, "k_gain": k_gain, "w_mo": w_mo,
            "w_ao": w_ao, "w_out": w_out, "norm_final": norm_final}


def reference(x_prompt, x_sample, w_in, b_if, norm_pre, q_gain, k_gain, w_mo, w_ao, w_out, norm_final):
    xp, xs = x_prompt, x_sample
    for l in range(DEPTH):
        xp = _encoder_layer(xp, w_in[l], b_if[l], norm_pre[l], q_gain[l], k_gain[l], w_mo[l], w_ao[l], w_out[l])
        xs = _encoder_layer(xs, w_in[l], b_if[l], norm_pre[l], q_gain[l], k_gain[l], w_mo[l], w_ao[l], w_out[l])
    y_prompt = _rmsnorm(xp, norm_final)
    y_sample = _rmsnorm(xs, norm_final)
    return (y_prompt, y_sample)
```

```python
import functools

import numpy as np
import jax
import jax.numpy as jnp
from jax import lax
from jax.experimental import pallas as pl
from jax.experimental.pallas import tpu as pltpu

F32 = jnp.float32
BF16 = jnp.bfloat16

D_MODEL = 1024
N_HEADS_M = 4
DH_M = 256
W_M = N_HEADS_M * DH_M
CHUNK = 64
N_HEADS_A = 16
N_KV_A = 4
GROUP_A = N_HEADS_A // N_KV_A
DH_A = 64
W_A = N_HEADS_A * DH_A
W_KV = N_KV_A * DH_A
GRID_W = 64
AXIS_DIM = DH_A // 2
ROPE_THETA = 10000.0
EPS = 1e-6

LANES = 128
VT_ROWS = 80
VMEM_LIMIT = 56 * 1024 * 1024


def _cparams(sem):
    return pltpu.CompilerParams(dimension_semantics=sem, vmem_limit_bytes=VMEM_LIMIT)


def _rms_bf16(x, g):
    ms = jnp.mean(x * x, axis=-1, keepdims=True)
    return (x * lax.rsqrt(ms + EPS) * g).astype(BF16)


def _dot(a, b):
    return jnp.dot(a, b, preferred_element_type=F32)


def _dot_nt(a, b):
    return lax.dot_general(a, b, (((1,), (1,)), ((), ())), preferred_element_type=F32)


def _mproj_kernel(x_ref, g_ref, w_ref, wif_ref, q_ref, k_ref, v_ref, gate_ref, if_ref):
    h = _rms_bf16(x_ref[...], g_ref[...])

    def proj(c):
        return _dot(h, w_ref[:, c * W_M:(c + 1) * W_M])

    q_ref[...] = proj(0).astype(BF16)
    k_ref[...] = (proj(1) * (DH_M ** -0.5)).astype(BF16)
    v_ref[...] = proj(2).astype(BF16)
    o = proj(3)
    z = proj(4)
    gate_ref[...] = (jax.nn.sigmoid(o) * (z * jax.nn.sigmoid(z))).astype(BF16)
    if_ref[...] = _dot(h, wif_ref[...])


def _mproj(x2, g, w, wif, tm):
    t = x2.shape[0]
    tok = lambda i: (i, 0)
    fixed = lambda i: (0, 0)
    big = jax.ShapeDtypeStruct((t, W_M), BF16)
    return pl.pallas_call(
        _mproj_kernel,
        grid=(t // tm,),
        in_specs=[pl.BlockSpec((tm, D_MODEL), tok), pl.BlockSpec((1, D_MODEL), fixed),
                  pl.BlockSpec((D_MODEL, 5 * W_M), fixed), pl.BlockSpec((D_MODEL, LANES), fixed)],
        out_specs=[pl.BlockSpec((tm, W_M), tok)] * 4 + [pl.BlockSpec((tm, LANES), tok)],
        out_shape=[big, big, big, big, jax.ShapeDtypeStruct((t, LANES), F32)],
        compiler_params=_cparams(("parallel",)),
        name="mproj",
    )(x2, g, w, wif)


def _aproj_kernel(x_ref, g_ref, w_ref, ones_ref, qg_ref, kg_ref, cos_ref, sa_ref, sb_ref,
                  q_ref, k_ref, v_ref):
    h = _rms_bf16(x_ref[...], g_ref[...])
    cos, sa, sb = cos_ref[...], sa_ref[...], sb_ref[...]
    ones = ones_ref[...]

    def norm_rope(p, gain, scale):
        ss = _dot((p * p).astype(BF16), ones)
        y = p * lax.rsqrt(ss * (1.0 / DH_A) + EPS) * gain
        outs = []
        for c in range(p.shape[1] // LANES):
            yc = y[:, c * LANES:(c + 1) * LANES]
            up = pltpu.roll(yc, LANES - AXIS_DIM // 2, 1)
            dn = pltpu.roll(yc, AXIS_DIM // 2, 1)
            outs.append(((yc * cos + up * sa + dn * sb) * scale).astype(BF16))
        return outs

    qg = qg_ref[...]
    kg = kg_ref[...]
    pq = _dot(h, w_ref[:, :W_A])
    for c in range(W_A // 256):
        o = norm_rope(pq[:, c * 256:(c + 1) * 256], qg, DH_A ** -0.5)
        q_ref[:, c * 256:c * 256 + LANES] = o[0]
        q_ref[:, c * 256 + LANES:(c + 1) * 256] = o[1]
    pkv = _dot(h, w_ref[:, W_A:])
    o = norm_rope(pkv[:, :W_KV], kg, 1.0)
    k_ref[:, :LANES] = o[0]
    k_ref[:, LANES:] = o[1]
    v_ref[...] = pkv[:, W_KV:].astype(BF16)


def _aproj(x2, g, w, ones, qg, kg, cos, sa, sb, tm, seq):
    t = x2.shape[0]
    nt = seq // tm
    tok = lambda i: (i, 0)
    fixed = lambda i: (0, 0)
    pos = lambda i: (i % nt, 0)
    return pl.pallas_call(
        _aproj_kernel,
        grid=(t // tm,),
        in_specs=[pl.BlockSpec((tm, D_MODEL), tok), pl.BlockSpec((1, D_MODEL), fixed),
                  pl.BlockSpec((D_MODEL, W_A + 2 * W_KV), fixed), pl.BlockSpec((256, 256), fixed),
                  pl.BlockSpec((1, 256), fixed), pl.BlockSpec((1, 256), fixed),
                  pl.BlockSpec((tm, LANES), pos), pl.BlockSpec((tm, LANES), pos), pl.BlockSpec((tm, LANES), pos)],
        out_specs=[pl.BlockSpec((tm, W_A), tok), pl.BlockSpec((tm, W_KV), tok), pl.BlockSpec((tm, W_KV), tok)],
        out_shape=[jax.ShapeDtypeStruct((t, W_A), BF16), jax.ShapeDtypeStruct((t, W_KV), BF16),
                   jax.ShapeDtypeStruct((t, W_KV), BF16)],
        compiler_params=_cparams(("parallel",)),
        name="aproj",
    )(x2, g, w, ones, qg, kg, cos, sa, sb)


def _mlstm_chunk(q, k, v, kt, a_row, lf_row, m, ct, n, rev):
    ti = lax.broadcasted_iota(jnp.int32, (CHUNK, CHUNK), 0)
    si = lax.broadcasted_iota(jnp.int32, (CHUNK, CHUNK), 1)
    valid = (si >= ti) if rev else (si <= ti)
    eye = si == ti
    last = 0 if rev else CHUNK - 1

    cm = jnp.max(jnp.where(valid, a_row, -jnp.inf), axis=1, keepdims=True)
    mc = jnp.maximum(cm, m)
    bcol = jnp.sum(jnp.where(valid, lf_row, 0.0), axis=1, keepdims=True)
    acol = jnp.sum(jnp.where(eye, a_row, 0.0), axis=1, keepdims=True)
    dmat = jnp.where(valid, jnp.exp(a_row - mc), 0.0)
    inter = jnp.exp(m - mc)
    emt = jnp.exp(-(bcol + mc))
    mlast = mc[last:last + 1]
    m_new = bcol[last:last + 1] + mlast
    decay = jnp.exp(m - mlast)
    ws = jnp.exp(acol - mlast)

    qf = q.astype(F32)
    sd = _dot_nt(q, k) * dmat
    den = jnp.sum(sd, axis=1, keepdims=True) + inter * jnp.sum(qf * n, axis=1, keepdims=True)
    num = _dot(sd.astype(BF16), v) + inter * _dot(q, ct.astype(BF16))
    h = num * (1.0 / jnp.maximum(jnp.abs(den), emt))

    vw = (v.astype(F32) * ws).astype(BF16)
    ct_new = decay * ct + _dot(kt, vw)
    n_new = decay * n + jnp.sum(k.astype(F32) * ws, axis=0, keepdims=True)
    return h, m_new, ct_new, n_new


def _mlstm_kernel(bias_ref,
                  qf_ref, kf_ref, vf_ref, ktf_ref, gf_ref,
                  qb_ref, kb_ref, vb_ref, ktb_ref, gb_ref,
                  hf_ref, hb_ref,
                  ct_scr, n_scr, m_scr, a_scr, lf_scr, *, ncb):
    head = pl.program_id(1)

    @pl.when(pl.program_id(2) == 0)
    def _():
        ct_scr[...] = jnp.zeros_like(ct_scr)
        n_scr[...] = jnp.zeros_like(n_scr)
        m_scr[...] = jnp.zeros_like(m_scr)

    ri = lax.broadcasted_iota(jnp.int32, (CHUNK, CHUNK), 0)
    ci = lax.broadcasted_iota(jnp.int32, (CHUNK, CHUNK), 1)
    for d, g_ref in ((0, gf_ref), (1, gb_ref)):
        ig = g_ref[2 * d] + bias_ref[head, 2 * d]
        xf = g_ref[2 * d + 1] + bias_ref[head, 2 * d + 1]
        lf = jnp.minimum(xf, 0.0) - jnp.log(1.0 + jnp.exp(-jnp.abs(xf)))
        cmat = jnp.where((ri >= ci) if d else (ri <= ci), 1.0, 0.0).astype(BF16)
        h1 = lf.astype(BF16)
        r1 = lf - h1.astype(F32)
        h2 = r1.astype(BF16)
        h3 = (r1 - h2.astype(F32)).astype(BF16)
        b = _dot(h1, cmat) + _dot(h2, cmat) + _dot(h3, cmat)
        a_scr[d] = ig - b
        lf_scr[d] = lf

    dirs = ((qf_ref, kf_ref, vf_ref, ktf_ref, hf_ref), (qb_ref, kb_ref, vb_ref, ktb_ref, hb_ref))

    def step(d, c, m):
        q_ref, k_ref, v_ref, kt_ref, h_ref = dirs[d]
        rows = pl.ds(pl.multiple_of(c * CHUNK, CHUNK), CHUNK)
        h, m_new, ct_new, n_new = _mlstm_chunk(
            q_ref[rows, :], k_ref[rows, :], v_ref[rows, :], kt_ref[c],
            a_scr[d, pl.ds(c, 1), :], lf_scr[d, pl.ds(c, 1), :], m, ct_scr[d], n_scr[d], bool(d))
        h_ref[rows, :] = h.astype(BF16)
        ct_scr[d] = ct_new
        n_scr[d] = n_new
        return m_new

    def body(i, carry):
        return step(0, i, carry[0]), step(1, ncb - 1 - i, carry[1])

    mf, mb = lax.fori_loop(0, ncb, body, (m_scr[0:1, 0:1], m_scr[1:2, 0:1]))
    m_scr[0:1, :] = jnp.broadcast_to(mf, (1, LANES))
    m_scr[1:2, :] = jnp.broadcast_to(mb, (1, LANES))


def _mlstm(bias, q, k, v, kt, gates, sb):
    bsz, seq, _ = q.shape
    nb = seq // sb
    ncb = sb // CHUNK
    fwd = lambda b, h, j: (b, j, h)
    bwd = lambda b, h, j: (b, nb - 1 - j, h)
    fwd5 = lambda b, h, j: (b, h, j, 0, 0)
    bwd5 = lambda b, h, j: (b, h, nb - 1 - j, 0, 0)
    gfwd = lambda b, h, j: (b, h, 0, j, 0)
    gbwd = lambda b, h, j: (b, h, 0, nb - 1 - j, 0)
    tok = lambda im: pl.BlockSpec((None, sb, DH_M), im)
    ktb = lambda im: pl.BlockSpec((None, None, ncb, DH_M, CHUNK), im)
    gb = lambda im: pl.BlockSpec((None, None, 4, ncb, CHUNK), im)
    out = jax.ShapeDtypeStruct((bsz, seq, W_M), BF16)
    return pl.pallas_call(
        functools.partial(_mlstm_kernel, ncb=ncb),
        grid=(bsz, N_HEADS_M, nb),
        in_specs=[pl.BlockSpec(memory_space=pltpu.SMEM),
                  tok(fwd), tok(fwd), tok(fwd), ktb(fwd5), gb(gfwd),
                  tok(bwd), tok(bwd), tok(bwd), ktb(bwd5), gb(gbwd)],
        out_specs=[tok(fwd), tok(bwd)],
        out_shape=[out, out],
        scratch_shapes=[pltpu.VMEM((2, DH_M, DH_M), F32), pltpu.VMEM((2, 1, DH_M), F32),
                        pltpu.VMEM((8, LANES), F32), pltpu.VMEM((2, ncb, CHUNK), F32),
                        pltpu.VMEM((2, ncb, CHUNK), F32)],
        compiler_params=_cparams(("parallel", "parallel", "arbitrary")),
        name="mlstm",
    )(bias, q, k, v, kt, gates, q, k, v, kt, gates)


def _attn_kernel(qt_ref, k_ref, vt_ref, eye_ref, o_ref, *, nk):
    tq = qt_ref.shape[-1]
    qs = [qt_ref[g] for g in range(GROUP_A)]

    def body(j, carry):
        kj = k_ref[j]
        vj = vt_ref[j]
        out = []
        for g in range(GROUP_A):
            m, acc = carry[g]
            st = _dot(kj, qs[g])
            m_new = jnp.maximum(m, jnp.max(st, axis=0, keepdims=True))
            alpha = jnp.exp(m - m_new)
            pt = jnp.exp(st - m_new).astype(BF16)
            out.append((m_new, alpha * acc + _dot(vj, pt)))
        return tuple(out)

    init = tuple((jnp.full((1, tq), -jnp.inf, F32), jnp.zeros((VT_ROWS, tq), F32)) for _ in range(GROUP_A))
    res = lax.fori_loop(0, nk, body, init)
    ot = jnp.concatenate(
        [(acc[:DH_A] * (1.0 / acc[DH_A:DH_A + 1])).astype(BF16) for _, acc in res], axis=0)
    o_ref[...] = _dot_nt(eye_ref[...], ot).astype(BF16)


def _attn(qt, kc, vtc, eye, tq):
    bsz, _, _, seq = qt.shape
    nk, tk = kc.shape[2], kc.shape[3]
    return pl.pallas_call(
        functools.partial(_attn_kernel, nk=nk),
        grid=(bsz, N_KV_A, seq // tq),
        in_specs=[pl.BlockSpec((None, GROUP_A, DH_A, tq), lambda b, h, i: (b, h, 0, i)),
                  pl.BlockSpec((None, None, nk, tk, DH_A), lambda b, h, i: (b, h, 0, 0, 0)),
                  pl.BlockSpec((None, None, nk, VT_ROWS, tk), lambda b, h, i: (b, h, 0, 0, 0)),
                  pl.BlockSpec((tq, tq), lambda b, h, i: (0, 0))],
        out_specs=pl.BlockSpec((None, tq, GROUP_A * DH_A), lambda b, h, i: (b, i, h)),
        out_shape=jax.ShapeDtypeStruct((bsz, seq, W_A), BF16),
        compiler_params=_cparams(("parallel", "parallel", "arbitrary")),
        name="attn",
    )(qt, kc, vtc, eye)


def _final_kernel(x_ref, hf_ref, hb_ref, gm_ref, att_ref, gpre_ref, gfin_ref,
                  wz_ref, wmo_ref, wao_ref, wout_ref, y_ref):
    x = x_ref[...]
    h = _rms_bf16(x, gpre_ref[...])
    z_a = _dot(h, wz_ref[:, :W_A])
    um = ((hf_ref[...].astype(F32) + hb_ref[...].astype(F32)) * gm_ref[...].astype(F32)).astype(BF16)
    ua = (att_ref[...].astype(F32) * (z_a * jax.nn.sigmoid(z_a))).astype(BF16)
    y_m = _dot(um, wmo_ref[...])
    y_a = _dot(ua, wao_ref[...])
    g_m = _dot(h, wz_ref[:, W_A:W_A + D_MODEL])
    g_a = _dot(h, wz_ref[:, W_A + D_MODEL:])
    merged = jax.nn.sigmoid(g_m) * y_m + jax.nn.sigmoid(g_a) * y_a
    out = x + _dot(merged.astype(BF16), wout_ref[...])
    ms = jnp.mean(out * out, axis=-1, keepdims=True)
    y_ref[...] = out * lax.rsqrt(ms + EPS) * gfin_ref[...]


def _final(x2, hf, hb, gm, att, gpre, gfin, wz, wmo, wao, wout, tm):
    t = x2.shape[0]
    tok = lambda i: (i, 0)
    fixed = lambda i: (0, 0)
    act = pl.BlockSpec((tm, D_MODEL), tok)
    sq = pl.BlockSpec((D_MODEL, D_MODEL), fixed)
    vec = pl.BlockSpec((1, D_MODEL), fixed)
    return pl.pallas_call(
        _final_kernel,
        grid=(t // tm,),
        in_specs=[act, act, act, act, act, vec, vec, pl.BlockSpec((D_MODEL, 3 * D_MODEL), fixed), sq, sq, sq],
        out_specs=act,
        out_shape=jax.ShapeDtypeStruct((t, D_MODEL), F32),
        compiler_params=_cparams(("parallel",)),
        name="final",
    )(x2, hf, hb, gm, att, gpre, gfin, wz, wmo, wao, wout)


def _rope_tables(seq):
    lane = np.arange(LANES) % DH_A
    axis = lane // AXIS_DIM
    idx = (lane % AXIS_DIM) % (AXIS_DIM // 2)
    lo = (lane % AXIS_DIM) < (AXIS_DIM // 2)
    freqs = ROPE_THETA ** (-jnp.arange(0, AXIS_DIM, 2, dtype=F32) / AXIS_DIM)
    s = jnp.arange(seq)
    pos = jnp.where(jnp.asarray(axis)[None, :] == 0, (s // GRID_W)[:, None], (s % GRID_W)[:, None])
    ang = pos.astype(F32) * freqs[jnp.asarray(idx)][None, :]
    cos, sin = jnp.cos(ang), jnp.sin(ang)
    lo = jnp.asarray(lo)[None, :]
    return cos, jnp.where(lo, -sin, 0.0), jnp.where(lo, 0.0, sin)


def _prep_weights(w_in, b_if, norm_pre, q_gain, k_gain, w_mo, w_ao, w_out, norm_final):
    c0 = 5 * W_M
    c1 = c0 + 4 * N_HEADS_M
    c2 = c1 + W_A + 2 * W_KV
    wb = w_in.astype(BF16)
    blk = np.arange(256) // DH_A
    return dict(
        w_m=wb[:, :c0],
        w_if=jnp.pad(wb[:, c0:c1], ((0, 0), (0, LANES - 4 * N_HEADS_M))),
        w_a=wb[:, c1:c2],
        w_z=wb[:, c2:],
        bias=jnp.transpose(b_if.astype(F32), (2, 0, 1)).reshape(N_HEADS_M, 4),
        g_pre=norm_pre.astype(F32).reshape(1, D_MODEL),
        g_fin=norm_final.astype(F32).reshape(1, D_MODEL),
        qg=jnp.tile(q_gain.astype(F32), 256 // DH_A).reshape(1, 256),
        kg=jnp.tile(k_gain.astype(F32), 256 // DH_A).reshape(1, 256),
        ones=jnp.asarray(blk[:, None] == blk[None, :], dtype=BF16),
        w_mo=w_mo.astype(BF16), w_ao=w_ao.astype(BF16), w_out=w_out.astype(BF16),
    )


def _tiles(seq):
    return min(512, seq), min(2048, seq), min(256, seq), min(512, seq)


def _layer(x, p):
    bsz, seq, _ = x.shape
    tm, sb, tq, tk = _tiles(seq)
    nc = seq // CHUNK
    nk = seq // tk
    x2 = x.reshape(bsz * seq, D_MODEL)

    q_m, k_m, v_m, gate_m, if_raw = _mproj(x2, p["g_pre"], p["w_m"], p["w_if"], tm)
    cos, sa, sb_tab = _rope_tables(seq)
    q_a, k_a, v_a = _aproj(x2, p["g_pre"], p["w_a"], p["ones"], p["qg"], p["kg"], cos, sa, sb_tab, tm, seq)

    shp = (bsz, seq, W_M)
    gates = if_raw[:, :4 * N_HEADS_M].reshape(bsz, nc, CHUNK, 4, N_HEADS_M)
    gates = jnp.transpose(gates, (0, 4, 3, 1, 2))
    kt = jnp.transpose(k_m.reshape(bsz, nc, CHUNK, N_HEADS_M, DH_M), (0, 3, 1, 4, 2))
    h_f, h_b = _mlstm(p["bias"], q_m.reshape(shp), k_m.reshape(shp), v_m.reshape(shp), kt, gates, sb)

    qt = jnp.transpose(q_a.reshape(bsz, seq, N_HEADS_A, DH_A), (0, 2, 3, 1))
    kc = jnp.transpose(k_a.reshape(bsz, nk, tk, N_KV_A, DH_A), (0, 3, 1, 2, 4))
    vt = jnp.transpose(v_a.reshape(bsz, nk, tk, N_KV_A, DH_A), (0, 3, 1, 4, 2))
    extra = jnp.zeros((bsz, N_KV_A, nk, VT_ROWS - DH_A, tk), BF16).at[:, :, :, 0, :].set(1.0)
    vtc = jnp.concatenate([vt, extra], axis=3)
    att = _attn(qt, kc, vtc, jnp.eye(tq, dtype=BF16), tq)

    y = _final(x2, h_f.reshape(-1, W_M), h_b.reshape(-1, W_M), gate_m, att.reshape(-1, W_A),
               p["g_pre"], p["g_fin"], p["w_z"], p["w_mo"], p["w_ao"], p["w_out"], min(256, seq))
    return y.reshape(bsz, seq, D_MODEL)


def kernel(x_prompt, x_sample, w_in, b_if, norm_pre, q_gain, k_gain, w_mo, w_ao, w_out, norm_final):
    assert w_in.shape[0] == 1, "single-layer model"
    p = _prep_weights(w_in[0], b_if[0], norm_pre[0], q_gain[0], k_gain[0], w_mo[0], w_ao[0], w_out[0], norm_final)
    return (_layer(x_prompt, p), _layer(x_sample, p))
```

```python
import functools

import numpy as np
import jax
import jax.numpy as jnp
from jax import lax
from jax.experimental import pallas as pl
from jax.experimental.pallas import tpu as pltpu

F32 = jnp.float32
BF16 = jnp.bfloat16

D_MODEL = 1024
N_HEADS_M = 4
DH_M = 256
W_M = N_HEADS_M * DH_M
CHUNK = 64
N_HEADS_A = 16
N_KV_A = 4
GROUP_A = N_HEADS_A // N_KV_A
DH_A = 64
W_A = N_HEADS_A * DH_A
W_KV = N_KV_A * DH_A
GRID_W = 64
AXIS_DIM = DH_A // 2
ROPE_THETA = 10000.0
EPS = 1e-6

LANES = 128
VT_ROWS = 80
MXU_DIM = 256
ATT_CHUNK = 128
LOG2E = 1.4426950408889634
VMEM_LIMIT = 56 * 1024 * 1024


def _cparams(sem):
    return pltpu.CompilerParams(dimension_semantics=sem, vmem_limit_bytes=VMEM_LIMIT)


def _rms_bf16(x, g):
    ms = jnp.mean(x * x, axis=-1, keepdims=True)
    return (x * lax.rsqrt(ms + EPS) * g).astype(BF16)


def _dot(a, b):
    return jnp.dot(a, b, preferred_element_type=F32)


def _dot_nt(a, b):
    return lax.dot_general(a, b, (((1,), (1,)), ((), ())), preferred_element_type=F32)


def _mproj_kernel(x_ref, g_ref, w_ref, wif_ref, q_ref, k_ref, v_ref, gate_ref, if_ref):
    h = _rms_bf16(x_ref[...], g_ref[...])

    def proj(c):
        return _dot(h, w_ref[:, c * W_M:(c + 1) * W_M])

    q_ref[...] = proj(0).astype(BF16)
    k_ref[...] = (proj(1) * (DH_M ** -0.5)).astype(BF16)
    v_ref[...] = proj(2).astype(BF16)
    o = proj(3)
    z = proj(4)
    gate_ref[...] = (jax.nn.sigmoid(o) * (z * jax.nn.sigmoid(z))).astype(BF16)
    if_ref[...] = _dot(h, wif_ref[...])


def _mproj(x2, g, w, wif, tm):
    t = x2.shape[0]
    tok = lambda i: (i, 0)
    fixed = lambda i: (0, 0)
    big = jax.ShapeDtypeStruct((t, W_M), BF16)
    return pl.pallas_call(
        _mproj_kernel,
        grid=(t // tm,),
        in_specs=[pl.BlockSpec((tm, D_MODEL), tok), pl.BlockSpec((1, D_MODEL), fixed),
                  pl.BlockSpec((D_MODEL, 5 * W_M), fixed), pl.BlockSpec((D_MODEL, LANES), fixed)],
        out_specs=[pl.BlockSpec((tm, W_M), tok)] * 4 + [pl.BlockSpec((tm, LANES), tok)],
        out_shape=[big, big, big, big, jax.ShapeDtypeStruct((t, LANES), F32)],
        compiler_params=_cparams(("parallel",)),
        name="mproj",
    )(x2, g, w, wif)


def _aproj_kernel(x_ref, g_ref, w_ref, ones_ref, qg_ref, kg_ref, cos_ref, sa_ref, sb_ref,
                  q_ref, k_ref, v_ref):
    h = _rms_bf16(x_ref[...], g_ref[...])
    cos, sa, sb = cos_ref[...], sa_ref[...], sb_ref[...]
    ones = ones_ref[...]

    def norm_rope(p, gain, scale):
        ss = _dot((p * p).astype(BF16), ones)
        y = p * lax.rsqrt(ss * (1.0 / DH_A) + EPS) * gain
        outs = []
        for c in range(p.shape[1] // LANES):
            yc = y[:, c * LANES:(c + 1) * LANES]
            up = pltpu.roll(yc, LANES - AXIS_DIM // 2, 1)
            dn = pltpu.roll(yc, AXIS_DIM // 2, 1)
            outs.append(((yc * cos + up * sa + dn * sb) * scale).astype(BF16))
        return outs

    qg = qg_ref[...]
    kg = kg_ref[...]
    pq = _dot(h, w_ref[:, :W_A])
    for c in range(W_A // 256):
        o = norm_rope(pq[:, c * 256:(c + 1) * 256], qg, LOG2E * DH_A ** -0.5)
        q_ref[:, c * 256:c * 256 + LANES] = o[0]
        q_ref[:, c * 256 + LANES:(c + 1) * 256] = o[1]
    pkv = _dot(h, w_ref[:, W_A:])
    o = norm_rope(pkv[:, :W_KV], kg, 1.0)
    k_ref[:, :LANES] = o[0]
    k_ref[:, LANES:] = o[1]
    v_ref[...] = pkv[:, W_KV:].astype(BF16)


def _aproj(x2, g, w, ones, qg, kg, cos, sa, sb, tm, seq):
    t = x2.shape[0]
    nt = seq // tm
    tok = lambda i: (i, 0)
    fixed = lambda i: (0, 0)
    pos = lambda i: (i % nt, 0)
    return pl.pallas_call(
        _aproj_kernel,
        grid=(t // tm,),
        in_specs=[pl.BlockSpec((tm, D_MODEL), tok), pl.BlockSpec((1, D_MODEL), fixed),
                  pl.BlockSpec((D_MODEL, W_A + 2 * W_KV), fixed), pl.BlockSpec((256, 256), fixed),
                  pl.BlockSpec((1, 256), fixed), pl.BlockSpec((1, 256), fixed),
                  pl.BlockSpec((tm, LANES), pos), pl.BlockSpec((tm, LANES), pos), pl.BlockSpec((tm, LANES), pos)],
        out_specs=[pl.BlockSpec((tm, W_A), tok), pl.BlockSpec((tm, W_KV), tok), pl.BlockSpec((tm, W_KV), tok)],
        out_shape=[jax.ShapeDtypeStruct((t, W_A), BF16), jax.ShapeDtypeStruct((t, W_KV), BF16),
                   jax.ShapeDtypeStruct((t, W_KV), BF16)],
        compiler_params=_cparams(("parallel",)),
        name="aproj",
    )(x2, g, w, ones, qg, kg, cos, sa, sb)


def _mlstm_chunk(q, k, v, kt, a_row, lf_row, m, ct, n, rev):
    ti = lax.broadcasted_iota(jnp.int32, (CHUNK, CHUNK), 0)
    si = lax.broadcasted_iota(jnp.int32, (CHUNK, CHUNK), 1)
    valid = (si >= ti) if rev else (si <= ti)
    eye = si == ti
    last = 0 if rev else CHUNK - 1

    cm = jnp.max(jnp.where(valid, a_row, -jnp.inf), axis=1, keepdims=True)
    mc = jnp.maximum(cm, m)
    bcol = jnp.sum(jnp.where(valid, lf_row, 0.0), axis=1, keepdims=True)
    acol = jnp.sum(jnp.where(eye, a_row, 0.0), axis=1, keepdims=True)
    dmat = jnp.where(valid, jnp.exp(a_row - mc), 0.0)
    inter = jnp.exp(m - mc)
    emt = jnp.exp(-(bcol + mc))
    mlast = mc[last:last + 1]
    m_new = bcol[last:last + 1] + mlast
    decay = jnp.exp(m - mlast)
    ws = jnp.exp(acol - mlast)

    qf = q.astype(F32)
    sd = _dot_nt(q, k) * dmat
    den = jnp.sum(sd, axis=1, keepdims=True) + inter * jnp.sum(qf * n, axis=1, keepdims=True)
    num = _dot(sd.astype(BF16), v) + inter * _dot(q, ct.astype(BF16))
    h = num * (1.0 / jnp.maximum(jnp.abs(den), emt))

    vw = (v.astype(F32) * ws).astype(BF16)
    ct_new = decay * ct + _dot(kt, vw)
    n_new = decay * n + jnp.sum(k.astype(F32) * ws, axis=0, keepdims=True)
    return h, m_new, ct_new, n_new


def _mlstm_kernel(bias_ref,
                  qf_ref, kf_ref, vf_ref, ktf_ref, gf_ref,
                  qb_ref, kb_ref, vb_ref, ktb_ref, gb_ref,
                  hf_ref, hb_ref,
                  ct_scr, n_scr, m_scr, a_scr, lf_scr, *, ncb):
    head = pl.program_id(1)

    @pl.when(pl.program_id(2) == 0)
    def _():
        ct_scr[...] = jnp.zeros_like(ct_scr)
        n_scr[...] = jnp.zeros_like(n_scr)
        m_scr[...] = jnp.zeros_like(m_scr)

    ri = lax.broadcasted_iota(jnp.int32, (CHUNK, CHUNK), 0)
    ci = lax.broadcasted_iota(jnp.int32, (CHUNK, CHUNK), 1)
    for d, g_ref in ((0, gf_ref), (1, gb_ref)):
        ig = g_ref[2 * d] + bias_ref[head, 2 * d]
        xf = g_ref[2 * d + 1] + bias_ref[head, 2 * d + 1]
        lf = jnp.minimum(xf, 0.0) - jnp.log(1.0 + jnp.exp(-jnp.abs(xf)))
        cmat = jnp.where((ri >= ci) if d else (ri <= ci), 1.0, 0.0).astype(BF16)
        h1 = lf.astype(BF16)
        r1 = lf - h1.astype(F32)
        h2 = r1.astype(BF16)
        h3 = (r1 - h2.astype(F32)).astype(BF16)
        b = _dot(h1, cmat) + _dot(h2, cmat) + _dot(h3, cmat)
        a_scr[d] = ig - b
        lf_scr[d] = lf

    dirs = ((qf_ref, kf_ref, vf_ref, ktf_ref, hf_ref), (qb_ref, kb_ref, vb_ref, ktb_ref, hb_ref))

    def step(d, c, m):
        q_ref, k_ref, v_ref, kt_ref, h_ref = dirs[d]
        rows = pl.ds(pl.multiple_of(c * CHUNK, CHUNK), CHUNK)
        h, m_new, ct_new, n_new = _mlstm_chunk(
            q_ref[rows, :], k_ref[rows, :], v_ref[rows, :], kt_ref[c],
            a_scr[d, pl.ds(c, 1), :], lf_scr[d, pl.ds(c, 1), :], m, ct_scr[d], n_scr[d], bool(d))
        h_ref[rows, :] = h.astype(BF16)
        ct_scr[d] = ct_new
        n_scr[d] = n_new
        return m_new

    def body(i, carry):
        return step(0, i, carry[0]), step(1, ncb - 1 - i, carry[1])

    mf, mb = lax.fori_loop(0, ncb, body, (m_scr[0:1, 0:1], m_scr[1:2, 0:1]))
    m_scr[0:1, :] = jnp.broadcast_to(mf, (1, LANES))
    m_scr[1:2, :] = jnp.broadcast_to(mb, (1, LANES))


def _mlstm(bias, q, k, v, kt, gates, sb):
    bsz, seq, _ = q.shape
    nb = seq // sb
    ncb = sb // CHUNK
    fwd = lambda b, h, j: (b, j, h)
    bwd = lambda b, h, j: (b, nb - 1 - j, h)
    fwd5 = lambda b, h, j: (b, h, j, 0, 0)
    bwd5 = lambda b, h, j: (b, h, nb - 1 - j, 0, 0)
    gfwd = lambda b, h, j: (b, h, 0, j, 0)
    gbwd = lambda b, h, j: (b, h, 0, nb - 1 - j, 0)
    tok = lambda im: pl.BlockSpec((None, sb, DH_M), im)
    ktb = lambda im: pl.BlockSpec((None, None, ncb, DH_M, CHUNK), im)
    gb = lambda im: pl.BlockSpec((None, None, 4, ncb, CHUNK), im)
    out = jax.ShapeDtypeStruct((bsz, seq, W_M), BF16)
    return pl.pallas_call(
        functools.partial(_mlstm_kernel, ncb=ncb),
        grid=(bsz, N_HEADS_M, nb),
        in_specs=[pl.BlockSpec(memory_space=pltpu.SMEM),
                  tok(fwd), tok(fwd), tok(fwd), ktb(fwd5), gb(gfwd),
                  tok(bwd), tok(bwd), tok(bwd), ktb(bwd5), gb(gbwd)],
        out_specs=[tok(fwd), tok(bwd)],
        out_shape=[out, out],
        scratch_shapes=[pltpu.VMEM((2, DH_M, DH_M), F32), pltpu.VMEM((2, 1, DH_M), F32),
                        pltpu.VMEM((8, LANES), F32), pltpu.VMEM((2, ncb, CHUNK), F32),
                        pltpu.VMEM((2, ncb, CHUNK), F32)],
        compiler_params=_cparams(("parallel", "parallel", "arbitrary")),
        name="mlstm",
    )(bias, q, k, v, kt, gates, q, k, v, kt, gates)


def _attn_kernel(qt_ref, k_ref, vt_ref, eye_ref, o_ref, st0, st1, pt0, pt1, *, nk):
    st_scr = (st0, st1)
    pt_scr = (pt0, pt1)
    tq = qt_ref.shape[-1]
    tk = k_ref.shape[1]
    qs = [qt_ref[g] for g in range(GROUP_A)]

    def tick(t, slot, do_pv, do_qk, do_sm, state):
        pmax, ms, alphas, accs = state
        n_pmax, n_ms, n_alphas, n_accs = list(pmax), list(ms), list(alphas), list(accs)
        for g in range(GROUP_A):
            if do_sm:
                m_new = jnp.maximum(ms[g], jnp.max(pmax[g], axis=0, keepdims=True))
                n_alphas[g] = jnp.exp2(ms[g] - m_new)
                n_ms[g] = m_new
            if do_pv:
                acc = alphas[g] * accs[g]
            pm = None
            for c in range(tk // ATT_CHUNK):
                rows = slice(c * ATT_CHUNK, (c + 1) * ATT_CHUNK)
                if do_qk:
                    st = _dot(k_ref[t, rows, :], qs[g])
                    st_scr[slot][g, rows, :] = st
                    cm = jnp.max(st.reshape(ATT_CHUNK // 8, 8, tq), axis=0)
                    pm = cm if pm is None else jnp.maximum(pm, cm)
                if do_sm:
                    pt_scr[1 - slot][g, rows, :] = jnp.exp2(st_scr[1 - slot][g, rows, :] - m_new).astype(BF16)
                if do_pv and (c + 1) * ATT_CHUNK % MXU_DIM == 0:
                    kt = (c + 1) * ATT_CHUNK // MXU_DIM - 1
                    cols = slice(kt * MXU_DIM, (kt + 1) * MXU_DIM)
                    acc = acc + _dot(vt_ref[t - 2, :, cols], pt_scr[slot][g, cols, :])
            if do_qk:
                n_pmax[g] = pm
            if do_pv:
                n_accs[g] = acc
        return tuple(n_pmax), tuple(n_ms), tuple(n_alphas), tuple(n_accs)

    neg = tuple(jnp.full((1, tq), -jnp.inf, F32) for _ in range(GROUP_A))
    neg8 = tuple(jnp.full((8, tq), -jnp.inf, F32) for _ in range(GROUP_A))
    state = (neg8, neg, neg, tuple(jnp.zeros((VT_ROWS, tq), F32) for _ in range(GROUP_A)))
    state = tick(0, 0, False, True, False, state)
    state = tick(1, 1, False, True, True, state)

    def body(i, state):
        state = tick(2 * i + 2, 0, True, True, True, state)
        return tick(2 * i + 3, 1, True, True, True, state)

    state = lax.fori_loop(0, (nk - 2) // 2, body, state)
    state = tick(nk, 0, True, False, True, state)
    state = tick(nk + 1, 1, True, False, False, state)
    ot = jnp.concatenate(
        [(acc[:DH_A] * (1.0 / acc[DH_A:DH_A + 1])).astype(BF16) for acc in state[3]], axis=0)
    o_ref[...] = _dot_nt(eye_ref[...], ot).astype(BF16)


def _attn(qt, kc, vtc, eye, tq):
    bsz, _, _, seq = qt.shape
    nk, tk = kc.shape[2], kc.shape[3]
    assert nk % 2 == 0 and tk % MXU_DIM == 0
    return pl.pallas_call(
        functools.partial(_attn_kernel, nk=nk),
        grid=(bsz, N_KV_A, seq // tq),
        in_specs=[pl.BlockSpec((None, GROUP_A, DH_A, tq), lambda b, h, i: (b, h, 0, i)),
                  pl.BlockSpec((None, None, nk, tk, DH_A), lambda b, h, i: (b, h, 0, 0, 0)),
                  pl.BlockSpec((None, None, nk, VT_ROWS, tk), lambda b, h, i: (b, h, 0, 0, 0)),
                  pl.BlockSpec((tq, tq), lambda b, h, i: (0, 0))],
        out_specs=pl.BlockSpec((None, tq, GROUP_A * DH_A), lambda b, h, i: (b, i, h)),
        out_shape=jax.ShapeDtypeStruct((bsz, seq, W_A), BF16),
        scratch_shapes=[pltpu.VMEM((GROUP_A, tk, tq), F32), pltpu.VMEM((GROUP_A, tk, tq), F32),
                        pltpu.VMEM((GROUP_A, tk, tq), BF16), pltpu.VMEM((GROUP_A, tk, tq), BF16)],
        compiler_params=_cparams(("parallel", "parallel", "arbitrary")),
        name="attn",
    )(qt, kc, vtc, eye)


def _final_kernel(x_ref, hf_ref, hb_ref, gm_ref, att_ref, gpre_ref, gfin_ref,
                  wz_ref, wmo_ref, wao_ref, wout_ref, y_ref):
    x = x_ref[...]
    h = _rms_bf16(x, gpre_ref[...])
    z_a = _dot(h, wz_ref[:, :W_A])
    um = ((hf_ref[...].astype(F32) + hb_ref[...].astype(F32)) * gm_ref[...].astype(F32)).astype(BF16)
    ua = (att_ref[...].astype(F32) * (z_a * jax.nn.sigmoid(z_a))).astype(BF16)
    y_m = _dot(um, wmo_ref[...])
    y_a = _dot(ua, wao_ref[...])
    g_m = _dot(h, wz_ref[:, W_A:W_A + D_MODEL])
    g_a = _dot(h, wz_ref[:, W_A + D_MODEL:])
    merged = jax.nn.sigmoid(g_m) * y_m + jax.nn.sigmoid(g_a) * y_a
    out = x + _dot(merged.astype(BF16), wout_ref[...])
    ms = jnp.mean(out * out, axis=-1, keepdims=True)
    y_ref[...] = out * lax.rsqrt(ms + EPS) * gfin_ref[...]


def _final(x2, hf, hb, gm, att, gpre, gfin, wz, wmo, wao, wout, tm):
    t = x2.shape[0]
    tok = lambda i: (i, 0)
    fixed = lambda i: (0, 0)
    act = pl.BlockSpec((tm, D_MODEL), tok)
    sq = pl.BlockSpec((D_MODEL, D_MODEL), fixed)
    vec = pl.BlockSpec((1, D_MODEL), fixed)
    return pl.pallas_call(
        _final_kernel,
        grid=(t // tm,),
        in_specs=[act, act, act, act, act, vec, vec, pl.BlockSpec((D_MODEL, 3 * D_MODEL), fixed), sq, sq, sq],
        out_specs=act,
        out_shape=jax.ShapeDtypeStruct((t, D_MODEL), F32),
        compiler_params=_cparams(("parallel",)),
        name="final",
    )(x2, hf, hb, gm, att, gpre, gfin, wz, wmo, wao, wout)


def _rope_tables(seq):
    lane = np.arange(LANES) % DH_A
    axis = lane // AXIS_DIM
    idx = (lane % AXIS_DIM) % (AXIS_DIM // 2)
    lo = (lane % AXIS_DIM) < (AXIS_DIM // 2)
    freqs = ROPE_THETA ** (-jnp.arange(0, AXIS_DIM, 2, dtype=F32) / AXIS_DIM)
    s = jnp.arange(seq)
    pos = jnp.where(jnp.asarray(axis)[None, :] == 0, (s // GRID_W)[:, None], (s % GRID_W)[:, None])
    ang = pos.astype(F32) * freqs[jnp.asarray(idx)][None, :]
    cos, sin = jnp.cos(ang), jnp.sin(ang)
    lo = jnp.asarray(lo)[None, :]
    return cos, jnp.where(lo, -sin, 0.0), jnp.where(lo, 0.0, sin)


def _prep_weights(w_in, b_if, norm_pre, q_gain, k_gain, w_mo, w_ao, w_out, norm_final):
    c0 = 5 * W_M
    c1 = c0 + 4 * N_HEADS_M
    c2 = c1 + W_A + 2 * W_KV
    wb = w_in.astype(BF16)
    blk = np.arange(256) // DH_A
    return dict(
        w_m=wb[:, :c0],
        w_if=jnp.pad(wb[:, c0:c1], ((0, 0), (0, LANES - 4 * N_HEADS_M))),
        w_a=wb[:, c1:c2],
        w_z=wb[:, c2:],
        bias=jnp.transpose(b_if.astype(F32), (2, 0, 1)).reshape(N_HEADS_M, 4),
        g_pre=norm_pre.astype(F32).reshape(1, D_MODEL),
        g_fin=norm_final.astype(F32).reshape(1, D_MODEL),
        qg=jnp.tile(q_gain.astype(F32), 256 // DH_A).reshape(1, 256),
        kg=jnp.tile(k_gain.astype(F32), 256 // DH_A).reshape(1, 256),
        ones=jnp.asarray(blk[:, None] == blk[None, :], dtype=BF16),
        w_mo=w_mo.astype(BF16), w_ao=w_ao.astype(BF16), w_out=w_out.astype(BF16),
    )


def _tiles(seq):
    return min(512, seq), min(2048, seq), min(256, seq), min(512, seq)


def _layer(x, p):
    bsz, seq, _ = x.shape
    tm, sb, tq, tk = _tiles(seq)
    nc = seq // CHUNK
    nk = seq // tk
    x2 = x.reshape(bsz * seq, D_MODEL)

    q_m, k_m, v_m, gate_m, if_raw = _mproj(x2, p["g_pre"], p["w_m"], p["w_if"], tm)
    cos, sa, sb_tab = _rope_tables(seq)
    q_a, k_a, v_a = _aproj(x2, p["g_pre"], p["w_a"], p["ones"], p["qg"], p["kg"], cos, sa, sb_tab, tm, seq)

    shp = (bsz, seq, W_M)
    gates = if_raw[:, :4 * N_HEADS_M].reshape(bsz, nc, CHUNK, 4, N_HEADS_M)
    gates = jnp.transpose(gates, (0, 4, 3, 1, 2))
    kt = jnp.transpose(k_m.reshape(bsz, nc, CHUNK, N_HEADS_M, DH_M), (0, 3, 1, 4, 2))
    h_f, h_b = _mlstm(p["bias"], q_m.reshape(shp), k_m.reshape(shp), v_m.reshape(shp), kt, gates, sb)

    qt = jnp.transpose(q_a.reshape(bsz, seq, N_HEADS_A, DH_A), (0, 2, 3, 1))
    kc = jnp.transpose(k_a.reshape(bsz, nk, tk, N_KV_A, DH_A), (0, 3, 1, 2, 4))
    vt = jnp.transpose(v_a.reshape(bsz, nk, tk, N_KV_A, DH_A), (0, 3, 1, 4, 2))
    extra = jnp.zeros((bsz, N_KV_A, nk, VT_ROWS - DH_A, tk), BF16).at[:, :, :, 0, :].set(1.0)
    vtc = jnp.concatenate([vt, extra], axis=3)
    att = _attn(qt, kc, vtc, jnp.eye(tq, dtype=BF16), tq)

    y = _final(x2, h_f.reshape(-1, W_M), h_b.reshape(-1, W_M), gate_m, att.reshape(-1, W_A),
               p["g_pre"], p["g_fin"], p["w_z"], p["w_mo"], p["w_ao"], p["w_out"], min(256, seq))
    return y.reshape(bsz, seq, D_MODEL)


def kernel(x_prompt, x_sample, w_in, b_if, norm_pre, q_gain, k_gain, w_mo, w_ao, w_out, norm_final):
    assert w_in.shape[0] == 1, "single-layer model"
    p = _prep_weights(w_in[0], b_if[0], norm_pre[0], q_gain[0], k_gain[0], w_mo[0], w_ao[0], w_out[0], norm_final)
    return (_layer(x_prompt, p), _layer(x_sample, p))
```

```python
import functools

import numpy as np
import jax
import jax.numpy as jnp
from jax import lax
from jax.experimental import pallas as pl
from jax.experimental.pallas import tpu as pltpu

F32 = jnp.float32
BF16 = jnp.bfloat16

D_MODEL = 1024
N_HEADS_M = 4
DH_M = 256
W_M = N_HEADS_M * DH_M
CHUNK = 64
N_HEADS_A = 16
N_KV_A = 4
GROUP_A = N_HEADS_A // N_KV_A
DH_A = 64
W_A = N_HEADS_A * DH_A
W_KV = N_KV_A * DH_A
GRID_W = 64
AXIS_DIM = DH_A // 2
ROPE_THETA = 10000.0
EPS = 1e-6

LANES = 128
VT_ROWS = 80
MXU_DIM = 256
ATT_CHUNK = 128
LOG2E = 1.4426950408889634
MLSTM_UNROLL = 4
VMEM_LIMIT = 56 * 1024 * 1024


def _cparams(sem):
    return pltpu.CompilerParams(dimension_semantics=sem, vmem_limit_bytes=VMEM_LIMIT)


def _rms_bf16(x, g):
    ms = jnp.mean(x * x, axis=-1, keepdims=True)
    return (x * lax.rsqrt(ms + EPS) * g).astype(BF16)


def _dot(a, b):
    return jnp.dot(a, b, preferred_element_type=F32)


def _dot_nt(a, b):
    return lax.dot_general(a, b, (((1,), (1,)), ((), ())), preferred_element_type=F32)


def _mproj_kernel(x_ref, g_ref, w_ref, wif_ref, q_ref, k_ref, v_ref, gate_ref, if_ref):
    h = _rms_bf16(x_ref[...], g_ref[...])

    def proj(c):
        return _dot(h, w_ref[:, c * W_M:(c + 1) * W_M])

    q_ref[...] = proj(0).astype(BF16)
    k_ref[...] = (proj(1) * (DH_M ** -0.5)).astype(BF16)
    v_ref[...] = proj(2).astype(BF16)
    o = proj(3)
    z = proj(4)
    gate_ref[...] = (jax.nn.sigmoid(o) * (z * jax.nn.sigmoid(z))).astype(BF16)
    if_ref[...] = _dot(h, wif_ref[...])


def _mproj(x2, g, w, wif, tm):
    t = x2.shape[0]
    tok = lambda i: (i, 0)
    fixed = lambda i: (0, 0)
    big = jax.ShapeDtypeStruct((t, W_M), BF16)
    return pl.pallas_call(
        _mproj_kernel,
        grid=(t // tm,),
        in_specs=[pl.BlockSpec((tm, D_MODEL), tok), pl.BlockSpec((1, D_MODEL), fixed),
                  pl.BlockSpec((D_MODEL, 5 * W_M), fixed), pl.BlockSpec((D_MODEL, LANES), fixed)],
        out_specs=[pl.BlockSpec((tm, W_M), tok)] * 4 + [pl.BlockSpec((tm, LANES), tok)],
        out_shape=[big, big, big, big, jax.ShapeDtypeStruct((t, LANES), F32)],
        compiler_params=_cparams(("parallel",)),
        name="mproj",
    )(x2, g, w, wif)


def _aproj_kernel(x_ref, g_ref, wqt_ref, wkp_ref, wvt_ref, ones_ref, qa_ref, qb_ref, ka_ref, kup_ref, kdn_ref,
                  qt_ref, k_ref, vt_ref):
    tm = x_ref.shape[0]
    h = _rms_bf16(x_ref[...], g_ref[...])
    half = AXIS_DIM // 2

    qt = _dot_nt(wqt_ref[...], h).reshape(N_HEADS_A, DH_A, tm)
    xh = qt * lax.rsqrt(jnp.mean(qt * qt, axis=1, keepdims=True) + EPS)
    x5 = xh.reshape(N_HEADS_A, 2, 2, half, tm)
    partner = jnp.concatenate([x5[:, :, 1:2], x5[:, :, 0:1]], axis=2).reshape(N_HEADS_A, DH_A, tm)
    qt_ref[...] = (xh * qa_ref[...] + partner * qb_ref[...]).astype(BF16)

    kp = _dot(h, wkp_ref[...])
    ones = ones_ref[...]
    ka, kup, kdn = ka_ref[...], kup_ref[...], kdn_ref[...]
    for c in range(N_KV_A // 2):
        blk = kp[:, c * 256:(c + 1) * 256]
        ss = _dot((blk * blk).astype(BF16), ones)
        y = blk * lax.rsqrt(ss * (1.0 / DH_A) + EPS)
        for j in range(2):
            yc = y[:, j * LANES:(j + 1) * LANES]
            up = pltpu.roll(yc, LANES - half, 1)
            dn = pltpu.roll(yc, half, 1)
            k_ref[2 * c + j] = (yc * ka + up * kup + dn * kdn)[:, :DH_A].astype(BF16)

    vt = _dot_nt(wvt_ref[...], h)
    row = lax.broadcasted_iota(jnp.int32, (VT_ROWS - DH_A, tm), 0)
    extra = jnp.where(row == 0, 1.0, 0.0).astype(BF16)
    for hh in range(N_KV_A):
        vt_ref[hh, :DH_A, :] = vt[hh * DH_A:(hh + 1) * DH_A, :].astype(BF16)
        vt_ref[hh, DH_A:, :] = extra


def _aproj(x3, g, wqt, wkp, wvt, ones, qa, qb, ka, kup, kdn, tk):
    bsz, seq, _ = x3.shape
    nk = seq // tk
    fixed = lambda b, j: (0, 0)
    return pl.pallas_call(
        _aproj_kernel,
        grid=(bsz, nk),
        in_specs=[pl.BlockSpec((None, tk, D_MODEL), lambda b, j: (b, j, 0)), pl.BlockSpec((1, D_MODEL), fixed),
                  pl.BlockSpec((W_A, D_MODEL), fixed), pl.BlockSpec((D_MODEL, N_KV_A * LANES), fixed),
                  pl.BlockSpec((W_KV, D_MODEL), fixed), pl.BlockSpec((256, 256), fixed),
                  pl.BlockSpec((DH_A, tk), lambda b, j: (0, j)), pl.BlockSpec((DH_A, tk), lambda b, j: (0, j)),
                  pl.BlockSpec((tk, LANES), lambda b, j: (j, 0)), pl.BlockSpec((tk, LANES), lambda b, j: (j, 0)),
                  pl.BlockSpec((tk, LANES), lambda b, j: (j, 0))],
        out_specs=[pl.BlockSpec((None, N_HEADS_A, DH_A, tk), lambda b, j: (b, 0, 0, j)),
                   pl.BlockSpec((None, N_KV_A, None, tk, DH_A), lambda b, j: (b, 0, j, 0, 0)),
                   pl.BlockSpec((None, N_KV_A, None, VT_ROWS, tk), lambda b, j: (b, 0, j, 0, 0))],
        out_shape=[jax.ShapeDtypeStruct((bsz, N_HEADS_A, DH_A, seq), BF16),
                   jax.ShapeDtypeStruct((bsz, N_KV_A, nk, tk, DH_A), BF16),
                   jax.ShapeDtypeStruct((bsz, N_KV_A, nk, VT_ROWS, tk), BF16)],
        compiler_params=_cparams(("parallel", "parallel")),
        name="aproj",
    )(x3, g, wqt, wkp, wvt, ones, qa, qb, ka, kup, kdn)


def _split3(x):
    h1 = x.astype(BF16)
    r1 = x - h1.astype(F32)
    h2 = r1.astype(BF16)
    return h1, h2, (r1 - h2.astype(F32)).astype(BF16)


def _mlstm_kernel(bias_ref,
                  qf_ref, kf_ref, vf_ref, ktf_ref, gf_ref,
                  qb_ref, kb_ref, vb_ref, ktb_ref, gb_ref,
                  hf_ref, hb_ref,
                  cta_scr, m_scr, a_scr, lf_scr, mc_scr, *, ncb):
    head = pl.program_id(1)

    @pl.when(pl.program_id(2) == 0)
    def _():
        cta_scr[...] = jnp.zeros_like(cta_scr)
        m_scr[...] = jnp.zeros_like(m_scr)

    ri = lax.broadcasted_iota(jnp.int32, (CHUNK, CHUNK), 0)
    ci = lax.broadcasted_iota(jnp.int32, (CHUNK, CHUNK), 1)
    ones_b = jnp.ones((CHUNK, LANES), BF16)

    for d, g_ref in ((0, gf_ref), (1, gb_ref)):
        last = 0 if d else CHUNK - 1
        xf = g_ref[2 * d + 1] + bias_ref[head, 2 * d + 1]
        lf = (jnp.minimum(xf, 0.0) - jnp.log(1.0 + jnp.exp(-jnp.abs(xf)))) * LOG2E
        cmat = jnp.where((ri >= ci) if d else (ri <= ci), 1.0, 0.0).astype(BF16)
        b = sum(_dot(t, cmat) for t in _split3(lf))
        a = (g_ref[2 * d] + bias_ref[head, 2 * d]) * LOG2E - b
        a_scr[d] = a
        lf_scr[d] = lf
        amax = jnp.broadcast_to(jnp.max(a, axis=1, keepdims=True), (ncb, LANES))
        blast = jnp.broadcast_to(b[:, last:last + 1], (ncb, LANES))
        m = m_scr[d:d + 1, :]
        for c in (range(ncb - 1, -1, -1) if d else range(ncb)):
            mc_scr[d, c:c + 1, :] = m
            m = blast[c:c + 1, :] + jnp.maximum(m, amax[c:c + 1, :])
        m_scr[d:d + 1, :] = m

    dirs = ((qf_ref, kf_ref, vf_ref, ktf_ref, hf_ref), (qb_ref, kb_ref, vb_ref, ktb_ref, hb_ref))

    def rows_of(c):
        return pl.ds(pl.multiple_of(c * CHUNK, CHUNK), CHUNK)

    def stage1(d, c):
        q_ref, k_ref = dirs[d][0], dirs[d][1]
        rows = rows_of(c)
        valid = (ci >= ri) if d else (ci <= ri)
        last = 0 if d else CHUNK - 1
        a_row = a_scr[d, pl.ds(c, 1), :]
        lf_row = lf_scr[d, pl.ds(c, 1), :]
        m = mc_scr[d, pl.ds(c, 1), 0:1]
        cm = jnp.max(jnp.where(valid, a_row, -jnp.inf), axis=1, keepdims=True)
        bcol = jnp.sum(jnp.where(valid, lf_row, 0.0), axis=1, keepdims=True)
        mc = jnp.maximum(cm, m)
        dmat = jnp.where(valid, jnp.exp2(a_row - mc), 0.0)
        inter = jnp.broadcast_to(jnp.exp2(m - mc), (CHUNK, LANES))
        emt = jnp.broadcast_to(jnp.exp2(-(bcol + mc)), (CHUNK, LANES))
        dec = jnp.broadcast_to(jnp.exp2(m - mc[last:last + 1]), (1, LANES))
        sd = (_dot_nt(q_ref[rows, :], k_ref[rows, :]) * dmat).astype(BF16)
        return sd, dmat[last:last + 1, :], inter, emt, dec

    def stage2(d, c, pre):
        sd, ws_row, inter, emt, dec = pre
        q_ref, _, v_ref, kt_ref, h_ref = dirs[d]
        rows = rows_of(c)
        q = q_ref[rows, :]
        v_aug = jnp.concatenate([v_ref[rows, :], ones_b], axis=1)
        cta = cta_scr[d]
        ktw = (kt_ref[c].astype(F32) * ws_row).astype(BF16)
        both = _dot(jnp.concatenate([sd, ktw], axis=0), v_aug)
        na = both[:CHUNK] + jnp.concatenate([inter, inter, inter], axis=1) * _dot(q, cta.astype(BF16))
        r = 1.0 / jnp.maximum(jnp.abs(na[:, DH_M:]), emt)
        h_ref[rows, :] = (na[:, :DH_M] * jnp.concatenate([r, r], axis=1)).astype(BF16)
        cta_scr[d] = jnp.concatenate([dec, dec, dec], axis=1) * cta + both[CHUNK:]

    def body(i, carry):
        nxt_f = stage1(0, jnp.minimum(i + 1, ncb - 1))
        nxt_b = stage1(1, jnp.maximum(ncb - 2 - i, 0))
        stage2(0, i, carry[0])
        stage2(1, ncb - 1 - i, carry[1])
        return nxt_f, nxt_b

    lax.fori_loop(0, ncb, body, (stage1(0, 0), stage1(1, ncb - 1)), unroll=MLSTM_UNROLL)


def _mlstm(bias, q, k, v, kt, gates, sb):
    bsz, seq, _ = q.shape
    nb = seq // sb
    ncb = sb // CHUNK
    fwd = lambda b, h, j: (b, j, h)
    bwd = lambda b, h, j: (b, nb - 1 - j, h)
    fwd5 = lambda b, h, j: (b, h, j, 0, 0)
    bwd5 = lambda b, h, j: (b, h, nb - 1 - j, 0, 0)
    gfwd = lambda b, h, j: (b, h, 0, j, 0)
    gbwd = lambda b, h, j: (b, h, 0, nb - 1 - j, 0)
    tok = lambda im: pl.BlockSpec((None, sb, DH_M), im)
    ktb = lambda im: pl.BlockSpec((None, None, ncb, DH_M, CHUNK), im)
    gb = lambda im: pl.BlockSpec((None, None, 4, ncb, CHUNK), im)
    out = jax.ShapeDtypeStruct((bsz, seq, W_M), BF16)
    return pl.pallas_call(
        functools.partial(_mlstm_kernel, ncb=ncb),
        grid=(bsz, N_HEADS_M, nb),
        in_specs=[pl.BlockSpec(memory_space=pltpu.SMEM),
                  tok(fwd), tok(fwd), tok(fwd), ktb(fwd5), gb(gfwd),
                  tok(bwd), tok(bwd), tok(bwd), ktb(bwd5), gb(gbwd)],
        out_specs=[tok(fwd), tok(bwd)],
        out_shape=[out, out],
        scratch_shapes=[pltpu.VMEM((2, DH_M, DH_M + LANES), F32), pltpu.VMEM((8, LANES), F32),
                        pltpu.VMEM((2, ncb, CHUNK), F32), pltpu.VMEM((2, ncb, CHUNK), F32),
                        pltpu.VMEM((2, ncb, LANES), F32)],
        compiler_params=_cparams(("parallel", "parallel", "arbitrary")),
        name="mlstm",
    )(bias, q, k, v, kt, gates, q, k, v, kt, gates)


def _attn_kernel(qt_ref, k_ref, vt_ref, eye_ref, o_ref, st0, st1, pt0, pt1, *, nk):
    st_scr = (st0, st1)
    pt_scr = (pt0, pt1)
    tq = qt_ref.shape[-1]
    tk = k_ref.shape[1]
    qs = [qt_ref[g] for g in range(GROUP_A)]

    def tick(t, slot, do_pv, do_qk, do_sm, state):
        pmax, ms, alphas, accs = state
        n_pmax, n_ms, n_alphas, n_accs = list(pmax), list(ms), list(alphas), list(accs)
        for g in range(GROUP_A):
            if do_sm:
                m_new = jnp.maximum(ms[g], jnp.max(pmax[g], axis=0, keepdims=True))
                n_alphas[g] = jnp.exp2(ms[g] - m_new)
                n_ms[g] = m_new
            if do_pv:
                acc = alphas[g] * accs[g]
            pm = None
            for c in range(tk // ATT_CHUNK):
                rows = slice(c * ATT_CHUNK, (c + 1) * ATT_CHUNK)
                if do_qk:
                    st = _dot(k_ref[t, rows, :], qs[g])
                    st_scr[slot][g, rows, :] = st
                    cm = jnp.max(st.reshape(ATT_CHUNK // 8, 8, tq), axis=0)
                    pm = cm if pm is None else jnp.maximum(pm, cm)
                if do_sm:
                    pt_scr[1 - slot][g, rows, :] = jnp.exp2(st_scr[1 - slot][g, rows, :] - m_new).astype(BF16)
                if do_pv and (c + 1) * ATT_CHUNK % MXU_DIM == 0:
                    kt = (c + 1) * ATT_CHUNK // MXU_DIM - 1
                    cols = slice(kt * MXU_DIM, (kt + 1) * MXU_DIM)
                    acc = acc + _dot(vt_ref[t - 2, :, cols], pt_scr[slot][g, cols, :])
            if do_qk:
                n_pmax[g] = pm
            if do_pv:
                n_accs[g] = acc
        return tuple(n_pmax), tuple(n_ms), tuple(n_alphas), tuple(n_accs)

    neg = tuple(jnp.full((1, tq), -jnp.inf, F32) for _ in range(GROUP_A))
    neg8 = tuple(jnp.full((8, tq), -jnp.inf, F32) for _ in range(GROUP_A))
    state = (neg8, neg, neg, tuple(jnp.zeros((VT_ROWS, tq), F32) for _ in range(GROUP_A)))
    state = tick(0, 0, False, True, False, state)
    state = tick(1, 1, False, True, True, state)

    def body(i, state):
        state = tick(2 * i + 2, 0, True, True, True, state)
        return tick(2 * i + 3, 1, True, True, True, state)

    state = lax.fori_loop(0, (nk - 2) // 2, body, state)
    state = tick(nk, 0, True, False, True, state)
    state = tick(nk + 1, 1, True, False, False, state)
    ot = jnp.concatenate(
        [(acc[:DH_A] * (1.0 / acc[DH_A:DH_A + 1])).astype(BF16) for acc in state[3]], axis=0)
    o_ref[...] = _dot_nt(eye_ref[...], ot).astype(BF16)


def _attn(qt, kc, vtc, eye, tq):
    bsz, _, _, seq = qt.shape
    nk, tk = kc.shape[2], kc.shape[3]
    assert nk % 2 == 0 and tk % MXU_DIM == 0
    return pl.pallas_call(
        functools.partial(_attn_kernel, nk=nk),
        grid=(bsz, N_KV_A, seq // tq),
        in_specs=[pl.BlockSpec((None, GROUP_A, DH_A, tq), lambda b, h, i: (b, h, 0, i)),
                  pl.BlockSpec((None, None, nk, tk, DH_A), lambda b, h, i: (b, h, 0, 0, 0)),
                  pl.BlockSpec((None, None, nk, VT_ROWS, tk), lambda b, h, i: (b, h, 0, 0, 0)),
                  pl.BlockSpec((tq, tq), lambda b, h, i: (0, 0))],
        out_specs=pl.BlockSpec((None, tq, GROUP_A * DH_A), lambda b, h, i: (b, i, h)),
        out_shape=jax.ShapeDtypeStruct((bsz, seq, W_A), BF16),
        scratch_shapes=[pltpu.VMEM((GROUP_A, tk, tq), F32), pltpu.VMEM((GROUP_A, tk, tq), F32),
                        pltpu.VMEM((GROUP_A, tk, tq), BF16), pltpu.VMEM((GROUP_A, tk, tq), BF16)],
        compiler_params=_cparams(("parallel", "parallel", "arbitrary")),
        name="attn",
    )(qt, kc, vtc, eye)


def _final_kernel(x_ref, hf_ref, hb_ref, gm_ref, att_ref, gpre_ref, gfin_ref,
                  wz_ref, wmo_ref, wao_ref, wout_ref, y_ref):
    x = x_ref[...]
    h = _rms_bf16(x, gpre_ref[...])
    z_a = _dot(h, wz_ref[:, :W_A])
    um = ((hf_ref[...].astype(F32) + hb_ref[...].astype(F32)) * gm_ref[...].astype(F32)).astype(BF16)
    ua = (att_ref[...].astype(F32) * (z_a * jax.nn.sigmoid(z_a))).astype(BF16)
    y_m = _dot(um, wmo_ref[...])
    y_a = _dot(ua, wao_ref[...])
    g_m = _dot(h, wz_ref[:, W_A:W_A + D_MODEL])
    g_a = _dot(h, wz_ref[:, W_A + D_MODEL:])
    merged = jax.nn.sigmoid(g_m) * y_m + jax.nn.sigmoid(g_a) * y_a
    out = x + _dot(merged.astype(BF16), wout_ref[...])
    ms = jnp.mean(out * out, axis=-1, keepdims=True)
    y_ref[...] = out * lax.rsqrt(ms + EPS) * gfin_ref[...]


def _final(x2, hf, hb, gm, att, gpre, gfin, wz, wmo, wao, wout, tm):
    t = x2.shape[0]
    tok = lambda i: (i, 0)
    fixed = lambda i: (0, 0)
    act = pl.BlockSpec((tm, D_MODEL), tok)
    sq = pl.BlockSpec((D_MODEL, D_MODEL), fixed)
    vec = pl.BlockSpec((1, D_MODEL), fixed)
    return pl.pallas_call(
        _final_kernel,
        grid=(t // tm,),
        in_specs=[act, act, act, act, act, vec, vec, pl.BlockSpec((D_MODEL, 3 * D_MODEL), fixed), sq, sq, sq],
        out_specs=act,
        out_shape=jax.ShapeDtypeStruct((t, D_MODEL), F32),
        compiler_params=_cparams(("parallel",)),
        name="final",
    )(x2, hf, hb, gm, att, gpre, gfin, wz, wmo, wao, wout)


def _rope_tables(seq, q_gain, k_gain):
    half = AXIS_DIM // 2
    d = np.arange(DH_A)
    axis = d // AXIS_DIM
    lo = (d % AXIS_DIM) < half
    idx = (d % AXIS_DIM) % half
    partner = np.where(lo, d + half, d - half)
    freqs = ROPE_THETA ** (-jnp.arange(0, AXIS_DIM, 2, dtype=F32) / AXIS_DIM)
    s = jnp.arange(seq)
    pos = jnp.where(jnp.asarray(axis)[:, None] == 0, (s // GRID_W)[None, :], (s % GRID_W)[None, :])
    ang = pos.astype(F32) * freqs[jnp.asarray(idx)][:, None]
    cos, sin = jnp.cos(ang), jnp.sin(ang)
    sign = jnp.where(jnp.asarray(lo), -1.0, 1.0)[:, None]

    def tables(gain, scale):
        g = gain.astype(F32) * scale
        return g[:, None] * cos, sign * g[jnp.asarray(partner)][:, None] * sin

    qa, qb = tables(q_gain, LOG2E * DH_A ** -0.5)
    ka, kb = tables(k_gain, 1.0)
    pad = lambda t: jnp.pad(t.T, ((0, 0), (0, LANES - DH_A)))
    lo_row = jnp.asarray(lo)[:, None]
    return qa, qb, pad(ka), pad(jnp.where(lo_row, kb, 0.0)), pad(jnp.where(lo_row, 0.0, kb))


def _prep_weights(w_in, b_if, norm_pre, w_mo, w_ao, w_out, norm_final):
    c0 = 5 * W_M
    c1 = c0 + 4 * N_HEADS_M
    c2 = c1 + W_A
    c3 = c2 + W_KV
    c4 = c3 + W_KV
    wb = w_in.astype(BF16)
    blk = np.arange(256) // LANES
    wk = wb[:, c2:c3].reshape(D_MODEL, N_KV_A, DH_A)
    return dict(
        w_m=wb[:, :c0],
        w_if=jnp.pad(wb[:, c0:c1], ((0, 0), (0, LANES - 4 * N_HEADS_M))),
        w_qt=wb[:, c1:c2].T,
        w_kp=jnp.pad(wk, ((0, 0), (0, 0), (0, LANES - DH_A))).reshape(D_MODEL, N_KV_A * LANES),
        w_vt=wb[:, c3:c4].T,
        w_z=wb[:, c4:],
        bias=jnp.transpose(b_if.astype(F32), (2, 0, 1)).reshape(N_HEADS_M, 4),
        g_pre=norm_pre.astype(F32).reshape(1, D_MODEL),
        g_fin=norm_final.astype(F32).reshape(1, D_MODEL),
        ones=jnp.asarray(blk[:, None] == blk[None, :], dtype=BF16),
        w_mo=w_mo.astype(BF16), w_ao=w_ao.astype(BF16), w_out=w_out.astype(BF16),
    )


def _tiles(seq):
    return min(512, seq), min(2048, seq), min(256, seq), min(512, seq)


def _layer(x, p, q_gain, k_gain):
    bsz, seq, _ = x.shape
    tm, sb, tq, tk = _tiles(seq)
    nc = seq // CHUNK
    x2 = x.reshape(bsz * seq, D_MODEL)

    q_m, k_m, v_m, gate_m, if_raw = _mproj(x2, p["g_pre"], p["w_m"], p["w_if"], tm)
    qt, kc, vtc = _aproj(x, p["g_pre"], p["w_qt"], p["w_kp"], p["w_vt"], p["ones"],
                         *_rope_tables(seq, q_gain, k_gain), tk)

    shp = (bsz, seq, W_M)
    gates = if_raw[:, :4 * N_HEADS_M].reshape(bsz, nc, CHUNK, 4, N_HEADS_M)
    gates = jnp.transpose(gates, (0, 4, 3, 1, 2))
    kt = jnp.transpose(k_m.reshape(bsz, nc, CHUNK, N_HEADS_M, DH_M), (0, 3, 1, 4, 2))
    h_f, h_b = _mlstm(p["bias"], q_m.reshape(shp), k_m.reshape(shp), v_m.reshape(shp), kt, gates, sb)

    att = _attn(qt, kc, vtc, jnp.eye(tq, dtype=BF16), tq)

    y = _final(x2, h_f.reshape(-1, W_M), h_b.reshape(-1, W_M), gate_m, att.reshape(-1, W_A),
               p["g_pre"], p["g_fin"], p["w_z"], p["w_mo"], p["w_ao"], p["w_out"], min(256, seq))
    return y.reshape(bsz, seq, D_MODEL)


def kernel(x_prompt, x_sample, w_in, b_if, norm_pre, q_gain, k_gain, w_mo, w_ao, w_out, norm_final):
    assert w_in.shape[0] == 1, "single-layer model"
    p = _prep_weights(w_in[0], b_if[0], norm_pre[0], w_mo[0], w_ao[0], w_out[0], norm_final)
    return (_layer(x_prompt, p, q_gain[0], k_gain[0]), _layer(x_sample, p, q_gain[0], k_gain[0]))
```

```python
import functools

import numpy as np
import jax
import jax.numpy as jnp
from jax import lax
from jax.experimental import pallas as pl
from jax.experimental.pallas import tpu as pltpu

F32 = jnp.float32
BF16 = jnp.bfloat16

D_MODEL = 1024
N_HEADS_M = 4
DH_M = 256
W_M = N_HEADS_M * DH_M
CHUNK = 64
N_HEADS_A = 16
N_KV_A = 4
GROUP_A = N_HEADS_A // N_KV_A
DH_A = 64
W_A = N_HEADS_A * DH_A
W_KV = N_KV_A * DH_A
GRID_W = 64
AXIS_DIM = DH_A // 2
ROPE_THETA = 10000.0
EPS = 1e-6

LANES = 128
VT_ROWS = 80
MXU_DIM = 256
ATT_CHUNK = 128
ATT_TICKS = 4
LOG2E = 1.4426950408889634
MLSTM_UNROLL = 4
VMEM_LIMIT = 56 * 1024 * 1024


def _cparams(sem):
    return pltpu.CompilerParams(dimension_semantics=sem, vmem_limit_bytes=VMEM_LIMIT)


def _rms_bf16(x, g):
    ms = jnp.mean(x * x, axis=-1, keepdims=True)
    return (x * lax.rsqrt(ms + EPS) * g).astype(BF16)


def _dot(a, b):
    return jnp.dot(a, b, preferred_element_type=F32)


def _dot_nt(a, b):
    return lax.dot_general(a, b, (((1,), (1,)), ((), ())), preferred_element_type=F32)


def _mproj_kernel(x_ref, g_ref, w_ref, wif_ref, q_ref, k_ref, v_ref, gate_ref, if_ref):
    h = _rms_bf16(x_ref[...], g_ref[...])

    def proj(c):
        return _dot(h, w_ref[:, c * W_M:(c + 1) * W_M])

    q_ref[...] = proj(0).astype(BF16)
    k_ref[...] = (proj(1) * (DH_M ** -0.5)).astype(BF16)
    v_ref[...] = proj(2).astype(BF16)
    o = proj(3)
    z = proj(4)
    gate_ref[...] = (jax.nn.sigmoid(o) * (z * jax.nn.sigmoid(z))).astype(BF16)
    if_ref[...] = _dot(h, wif_ref[...])


def _mproj(x2, g, w, wif, tm):
    t = x2.shape[0]
    tok = lambda i: (i, 0)
    fixed = lambda i: (0, 0)
    big = jax.ShapeDtypeStruct((t, W_M), BF16)
    return pl.pallas_call(
        _mproj_kernel,
        grid=(t // tm,),
        in_specs=[pl.BlockSpec((tm, D_MODEL), tok), pl.BlockSpec((1, D_MODEL), fixed),
                  pl.BlockSpec((D_MODEL, 5 * W_M), fixed), pl.BlockSpec((D_MODEL, LANES), fixed)],
        out_specs=[pl.BlockSpec((tm, W_M), tok)] * 4 + [pl.BlockSpec((tm, LANES), tok)],
        out_shape=[big, big, big, big, jax.ShapeDtypeStruct((t, LANES), F32)],
        compiler_params=_cparams(("parallel",)),
        name="mproj",
    )(x2, g, w, wif)


def _aproj_kernel(x_ref, g_ref, wqt_ref, wkp_ref, wvt_ref, ones_ref, qa_ref, qb_ref, ka_ref, kup_ref, kdn_ref,
                  qt_ref, k_ref, vt_ref):
    tm = x_ref.shape[0]
    h = _rms_bf16(x_ref[...], g_ref[...])
    half = AXIS_DIM // 2

    qt = _dot_nt(wqt_ref[...], h).reshape(N_HEADS_A, DH_A, tm)
    xh = qt * lax.rsqrt(jnp.mean(qt * qt, axis=1, keepdims=True) + EPS)
    x5 = xh.reshape(N_HEADS_A, 2, 2, half, tm)
    partner = jnp.concatenate([x5[:, :, 1:2], x5[:, :, 0:1]], axis=2).reshape(N_HEADS_A, DH_A, tm)
    qt_ref[...] = (xh * qa_ref[...] + partner * qb_ref[...]).astype(BF16)

    kp = _dot(h, wkp_ref[...])
    ones = ones_ref[...]
    ka, kup, kdn = ka_ref[...], kup_ref[...], kdn_ref[...]
    for c in range(N_KV_A // 2):
        blk = kp[:, c * 256:(c + 1) * 256]
        ss = _dot((blk * blk).astype(BF16), ones)
        y = blk * lax.rsqrt(ss * (1.0 / DH_A) + EPS)
        for j in range(2):
            yc = y[:, j * LANES:(j + 1) * LANES]
            up = pltpu.roll(yc, LANES - half, 1)
            dn = pltpu.roll(yc, half, 1)
            k_ref[2 * c + j] = (yc * ka + up * kup + dn * kdn)[:, :DH_A].astype(BF16)

    vt = _dot_nt(wvt_ref[...], h)
    row = lax.broadcasted_iota(jnp.int32, (VT_ROWS - DH_A, tm), 0)
    extra = jnp.where(row == 0, 1.0, 0.0).astype(BF16)
    for hh in range(N_KV_A):
        vt_ref[hh, :DH_A, :] = vt[hh * DH_A:(hh + 1) * DH_A, :].astype(BF16)
        vt_ref[hh, DH_A:, :] = extra


def _aproj(x3, g, wqt, wkp, wvt, ones, qa, qb, ka, kup, kdn, tk):
    bsz, seq, _ = x3.shape
    nk = seq // tk
    fixed = lambda b, j: (0, 0)
    return pl.pallas_call(
        _aproj_kernel,
        grid=(bsz, nk),
        in_specs=[pl.BlockSpec((None, tk, D_MODEL), lambda b, j: (b, j, 0)), pl.BlockSpec((1, D_MODEL), fixed),
                  pl.BlockSpec((W_A, D_MODEL), fixed), pl.BlockSpec((D_MODEL, N_KV_A * LANES), fixed),
                  pl.BlockSpec((W_KV, D_MODEL), fixed), pl.BlockSpec((256, 256), fixed),
                  pl.BlockSpec((DH_A, tk), lambda b, j: (0, j)), pl.BlockSpec((DH_A, tk), lambda b, j: (0, j)),
                  pl.BlockSpec((tk, LANES), lambda b, j: (j, 0)), pl.BlockSpec((tk, LANES), lambda b, j: (j, 0)),
                  pl.BlockSpec((tk, LANES), lambda b, j: (j, 0))],
        out_specs=[pl.BlockSpec((None, N_HEADS_A, DH_A, tk), lambda b, j: (b, 0, 0, j)),
                   pl.BlockSpec((None, N_KV_A, None, tk, DH_A), lambda b, j: (b, 0, j, 0, 0)),
                   pl.BlockSpec((None, N_KV_A, None, VT_ROWS, tk), lambda b, j: (b, 0, j, 0, 0))],
        out_shape=[jax.ShapeDtypeStruct((bsz, N_HEADS_A, DH_A, seq), BF16),
                   jax.ShapeDtypeStruct((bsz, N_KV_A, nk, tk, DH_A), BF16),
                   jax.ShapeDtypeStruct((bsz, N_KV_A, nk, VT_ROWS, tk), BF16)],
        compiler_params=_cparams(("parallel", "parallel")),
        name="aproj",
    )(x3, g, wqt, wkp, wvt, ones, qa, qb, ka, kup, kdn)


def _split3(x):
    h1 = x.astype(BF16)
    r1 = x - h1.astype(F32)
    h2 = r1.astype(BF16)
    return h1, h2, (r1 - h2.astype(F32)).astype(BF16)


def _mlstm_kernel(bias_ref,
                  qf_ref, kf_ref, vf_ref, ktf_ref, gf_ref,
                  qb_ref, kb_ref, vb_ref, ktb_ref, gb_ref,
                  hf_ref, hb_ref,
                  cta_scr, m_scr, a_scr, lf_scr, mc_scr, *, ncb):
    head = pl.program_id(1)

    @pl.when(pl.program_id(2) == 0)
    def _():
        cta_scr[...] = jnp.zeros_like(cta_scr)
        m_scr[...] = jnp.zeros_like(m_scr)

    ri = lax.broadcasted_iota(jnp.int32, (CHUNK, CHUNK), 0)
    ci = lax.broadcasted_iota(jnp.int32, (CHUNK, CHUNK), 1)
    ones_b = jnp.ones((CHUNK, LANES), BF16)

    for d, g_ref in ((0, gf_ref), (1, gb_ref)):
        last = 0 if d else CHUNK - 1
        xf = g_ref[2 * d + 1] + bias_ref[head, 2 * d + 1]
        lf = (jnp.minimum(xf, 0.0) - jnp.log(1.0 + jnp.exp(-jnp.abs(xf)))) * LOG2E
        cmat = jnp.where((ri >= ci) if d else (ri <= ci), 1.0, 0.0).astype(BF16)
        b = sum(_dot(t, cmat) for t in _split3(lf))
        a = (g_ref[2 * d] + bias_ref[head, 2 * d]) * LOG2E - b
        a_scr[d] = a
        lf_scr[d] = lf
        amax = jnp.broadcast_to(jnp.max(a, axis=1, keepdims=True), (ncb, LANES))
        blast = jnp.broadcast_to(b[:, last:last + 1], (ncb, LANES))
        m = m_scr[d:d + 1, :]
        for c in (range(ncb - 1, -1, -1) if d else range(ncb)):
            mc_scr[d, c:c + 1, :] = m
            m = blast[c:c + 1, :] + jnp.maximum(m, amax[c:c + 1, :])
        m_scr[d:d + 1, :] = m

    dirs = ((qf_ref, kf_ref, vf_ref, ktf_ref, hf_ref), (qb_ref, kb_ref, vb_ref, ktb_ref, hb_ref))

    def rows_of(c):
        return pl.ds(pl.multiple_of(c * CHUNK, CHUNK), CHUNK)

    def stage1(d, c):
        q_ref, k_ref = dirs[d][0], dirs[d][1]
        rows = rows_of(c)
        valid = (ci >= ri) if d else (ci <= ri)
        last = 0 if d else CHUNK - 1
        a_row = a_scr[d, pl.ds(c, 1), :]
        lf_row = lf_scr[d, pl.ds(c, 1), :]
        m = mc_scr[d, pl.ds(c, 1), 0:1]
        cm = jnp.max(jnp.where(valid, a_row, -jnp.inf), axis=1, keepdims=True)
        bcol = jnp.sum(jnp.where(valid, lf_row, 0.0), axis=1, keepdims=True)
        mc = jnp.maximum(cm, m)
        dmat = jnp.where(valid, jnp.exp2(a_row - mc), 0.0)
        inter = jnp.broadcast_to(jnp.exp2(m - mc), (CHUNK, LANES))
        emt = jnp.broadcast_to(jnp.exp2(-(bcol + mc)), (CHUNK, LANES))
        dec = jnp.broadcast_to(jnp.exp2(m - mc[last:last + 1]), (1, LANES))
        sd = (_dot_nt(q_ref[rows, :], k_ref[rows, :]) * dmat).astype(BF16)
        return sd, dmat[last:last + 1, :], inter, emt, dec

    def stage2(d, c, pre):
        sd, ws_row, inter, emt, dec = pre
        q_ref, _, v_ref, kt_ref, h_ref = dirs[d]
        rows = rows_of(c)
        q = q_ref[rows, :]
        v_aug = jnp.concatenate([v_ref[rows, :], ones_b], axis=1)
        cta = cta_scr[d]
        ktw = (kt_ref[c].astype(F32) * ws_row).astype(BF16)
        both = _dot(jnp.concatenate([sd, ktw], axis=0), v_aug)
        na = both[:CHUNK] + jnp.concatenate([inter, inter, inter], axis=1) * _dot(q, cta.astype(BF16))
        r = 1.0 / jnp.maximum(jnp.abs(na[:, DH_M:]), emt)
        h_ref[rows, :] = (na[:, :DH_M] * jnp.concatenate([r, r], axis=1)).astype(BF16)
        cta_scr[d] = jnp.concatenate([dec, dec, dec], axis=1) * cta + both[CHUNK:]

    def body(i, carry):
        nxt_f = stage1(0, jnp.minimum(i + 1, ncb - 1))
        nxt_b = stage1(1, jnp.maximum(ncb - 2 - i, 0))
        stage2(0, i, carry[0])
        stage2(1, ncb - 1 - i, carry[1])
        return nxt_f, nxt_b

    lax.fori_loop(0, ncb, body, (stage1(0, 0), stage1(1, ncb - 1)), unroll=MLSTM_UNROLL)


def _mlstm(bias, q, k, v, kt, gates, sb):
    bsz, seq, _ = q.shape
    nb = seq // sb
    ncb = sb // CHUNK
    fwd = lambda b, h, j: (b, j, h)
    bwd = lambda b, h, j: (b, nb - 1 - j, h)
    fwd5 = lambda b, h, j: (b, h, j, 0, 0)
    bwd5 = lambda b, h, j: (b, h, nb - 1 - j, 0, 0)
    gfwd = lambda b, h, j: (b, h, 0, j, 0)
    gbwd = lambda b, h, j: (b, h, 0, nb - 1 - j, 0)
    tok = lambda im: pl.BlockSpec((None, sb, DH_M), im)
    ktb = lambda im: pl.BlockSpec((None, None, ncb, DH_M, CHUNK), im)
    gb = lambda im: pl.BlockSpec((None, None, 4, ncb, CHUNK), im)
    out = jax.ShapeDtypeStruct((bsz, seq, W_M), BF16)
    return pl.pallas_call(
        functools.partial(_mlstm_kernel, ncb=ncb),
        grid=(bsz, N_HEADS_M, nb),
        in_specs=[pl.BlockSpec(memory_space=pltpu.SMEM),
                  tok(fwd), tok(fwd), tok(fwd), ktb(fwd5), gb(gfwd),
                  tok(bwd), tok(bwd), tok(bwd), ktb(bwd5), gb(gbwd)],
        out_specs=[tok(fwd), tok(bwd)],
        out_shape=[out, out],
        scratch_shapes=[pltpu.VMEM((2, DH_M, DH_M + LANES), F32), pltpu.VMEM((8, LANES), F32),
                        pltpu.VMEM((2, ncb, CHUNK), F32), pltpu.VMEM((2, ncb, CHUNK), F32),
                        pltpu.VMEM((2, ncb, LANES), F32)],
        compiler_params=_cparams(("parallel", "parallel", "arbitrary")),
        name="mlstm",
    )(bias, q, k, v, kt, gates, q, k, v, kt, gates)


def _attn_kernel(qt_ref, k_ref, vt_ref, eye_ref, o_ref, st0, st1, *, nk):
    st_scr = (st0, st1)
    tq = qt_ref.shape[-1]
    tk = k_ref.shape[1]
    qs = [qt_ref[g] for g in range(GROUP_A)]

    def tick(t, slot, do_qk, do_sm, state):
        pmax, ms, accs = state
        n_pmax, n_ms, n_accs = list(pmax), list(ms), list(accs)
        for g in range(GROUP_A):
            if do_sm:
                m_new = jnp.maximum(ms[g], jnp.max(pmax[g], axis=0, keepdims=True))
                acc = jnp.exp2(ms[g] - m_new) * accs[g]
                n_ms[g] = m_new
            pm = None
            pts = []
            for c in range(tk // ATT_CHUNK):
                rows = slice(c * ATT_CHUNK, (c + 1) * ATT_CHUNK)
                if do_qk:
                    st = _dot(k_ref[t, rows, :], qs[g])
                    st_scr[slot][g, rows, :] = st
                    cm = jnp.max(st.reshape(ATT_CHUNK // 8, 8, tq), axis=0)
                    pm = cm if pm is None else jnp.maximum(pm, cm)
                if do_sm:
                    pts.append(jnp.exp2(st_scr[1 - slot][g, rows, :] - m_new).astype(BF16))
                    if (c + 1) * ATT_CHUNK % MXU_DIM == 0:
                        kt = (c + 1) * ATT_CHUNK // MXU_DIM - 1
                        cols = slice(kt * MXU_DIM, (kt + 1) * MXU_DIM)
                        acc = acc + _dot(vt_ref[t - 1, :, cols], jnp.concatenate(pts, axis=0))
                        pts = []
            if do_qk:
                n_pmax[g] = pm
            if do_sm:
                n_accs[g] = acc
        return tuple(n_pmax), tuple(n_ms), tuple(n_accs)

    neg = tuple(jnp.full((1, tq), -jnp.inf, F32) for _ in range(GROUP_A))
    neg8 = tuple(jnp.full((8, tq), -jnp.inf, F32) for _ in range(GROUP_A))
    state = (neg8, neg, tuple(jnp.zeros((VT_ROWS, tq), F32) for _ in range(GROUP_A)))
    state = tick(0, 0, True, False, state)
    state = tick(1, 1, True, True, state)
    ntrip = (nk - 2) // ATT_TICKS

    def body(i, state):
        for u in range(ATT_TICKS):
            state = tick(ATT_TICKS * i + 2 + u, u % 2, True, True, state)
        return state

    state = lax.fori_loop(0, ntrip, body, state)
    for t in range(2 + ntrip * ATT_TICKS, nk):
        state = tick(t, t % 2, True, True, state)
    state = tick(nk, nk % 2, False, True, state)
    ot = jnp.concatenate(
        [(acc[:DH_A] * (1.0 / acc[DH_A:DH_A + 1])).astype(BF16) for acc in state[2]], axis=0)
    o_ref[...] = _dot_nt(eye_ref[...], ot).astype(BF16)


def _attn(qt, kc, vtc, eye, tq):
    bsz, _, _, seq = qt.shape
    nk, tk = kc.shape[2], kc.shape[3]
    assert nk >= 2 and tk % MXU_DIM == 0
    return pl.pallas_call(
        functools.partial(_attn_kernel, nk=nk),
        grid=(bsz, N_KV_A, seq // tq),
        in_specs=[pl.BlockSpec((None, GROUP_A, DH_A, tq), lambda b, h, i: (b, h, 0, i)),
                  pl.BlockSpec((None, None, nk, tk, DH_A), lambda b, h, i: (b, h, 0, 0, 0)),
                  pl.BlockSpec((None, None, nk, VT_ROWS, tk), lambda b, h, i: (b, h, 0, 0, 0)),
                  pl.BlockSpec((tq, tq), lambda b, h, i: (0, 0))],
        out_specs=pl.BlockSpec((None, tq, GROUP_A * DH_A), lambda b, h, i: (b, i, h)),
        out_shape=jax.ShapeDtypeStruct((bsz, seq, W_A), BF16),
        scratch_shapes=[pltpu.VMEM((GROUP_A, tk, tq), F32), pltpu.VMEM((GROUP_A, tk, tq), F32)],
        compiler_params=_cparams(("parallel", "parallel", "arbitrary")),
        name="attn",
    )(qt, kc, vtc, eye)


def _final_kernel(x_ref, hf_ref, hb_ref, gm_ref, att_ref, gpre_ref, gfin_ref,
                  wz_ref, wmo_ref, wao_ref, wout_ref, y_ref):
    x = x_ref[...]
    h = _rms_bf16(x, gpre_ref[...])
    z_a = _dot(h, wz_ref[:, :W_A])
    um = ((hf_ref[...].astype(F32) + hb_ref[...].astype(F32)) * gm_ref[...].astype(F32)).astype(BF16)
    ua = (att_ref[...].astype(F32) * (z_a * jax.nn.sigmoid(z_a))).astype(BF16)
    y_m = _dot(um, wmo_ref[...])
    y_a = _dot(ua, wao_ref[...])
    g_m = _dot(h, wz_ref[:, W_A:W_A + D_MODEL])
    g_a = _dot(h, wz_ref[:, W_A + D_MODEL:])
    merged = jax.nn.sigmoid(g_m) * y_m + jax.nn.sigmoid(g_a) * y_a
    out = x + _dot(merged.astype(BF16), wout_ref[...])
    ms = jnp.mean(out * out, axis=-1, keepdims=True)
    y_ref[...] = out * lax.rsqrt(ms + EPS) * gfin_ref[...]


def _final(x2, hf, hb, gm, att, gpre, gfin, wz, wmo, wao, wout, tm):
    t = x2.shape[0]
    tok = lambda i: (i, 0)
    fixed = lambda i: (0, 0)
    act = pl.BlockSpec((tm, D_MODEL), tok)
    sq = pl.BlockSpec((D_MODEL, D_MODEL), fixed)
    vec = pl.BlockSpec((1, D_MODEL), fixed)
    return pl.pallas_call(
        _final_kernel,
        grid=(t // tm,),
        in_specs=[act, act, act, act, act, vec, vec, pl.BlockSpec((D_MODEL, 3 * D_MODEL), fixed), sq, sq, sq],
        out_specs=act,
        out_shape=jax.ShapeDtypeStruct((t, D_MODEL), F32),
        compiler_params=_cparams(("parallel",)),
        name="final",
    )(x2, hf, hb, gm, att, gpre, gfin, wz, wmo, wao, wout)


def _rope_tables(seq, q_gain, k_gain):
    half = AXIS_DIM // 2
    d = np.arange(DH_A)
    axis = d // AXIS_DIM
    lo = (d % AXIS_DIM) < half
    idx = (d % AXIS_DIM) % half
    partner = np.where(lo, d + half, d - half)
    freqs = ROPE_THETA ** (-jnp.arange(0, AXIS_DIM, 2, dtype=F32) / AXIS_DIM)
    s = jnp.arange(seq)
    pos = jnp.where(jnp.asarray(axis)[:, None] == 0, (s // GRID_W)[None, :], (s % GRID_W)[None, :])
    ang = pos.astype(F32) * freqs[jnp.asarray(idx)][:, None]
    cos, sin = jnp.cos(ang), jnp.sin(ang)
    sign = jnp.where(jnp.asarray(lo), -1.0, 1.0)[:, None]

    def tables(gain, scale):
        g = gain.astype(F32) * scale
        return g[:, None] * cos, sign * g[jnp.asarray(partner)][:, None] * sin

    qa, qb = tables(q_gain, LOG2E * DH_A ** -0.5)
    ka, kb = tables(k_gain, 1.0)
    pad = lambda t: jnp.pad(t.T, ((0, 0), (0, LANES - DH_A)))
    lo_row = jnp.asarray(lo)[:, None]
    return qa, qb, pad(ka), pad(jnp.where(lo_row, kb, 0.0)), pad(jnp.where(lo_row, 0.0, kb))


def _prep_weights(w_in, b_if, norm_pre, w_mo, w_ao, w_out, norm_final):
    c0 = 5 * W_M
    c1 = c0 + 4 * N_HEADS_M
    c2 = c1 + W_A
    c3 = c2 + W_KV
    c4 = c3 + W_KV
    wb = w_in.astype(BF16)
    blk = np.arange(256) // LANES
    wk = wb[:, c2:c3].reshape(D_MODEL, N_KV_A, DH_A)
    return dict(
        w_m=wb[:, :c0],
        w_if=jnp.pad(wb[:, c0:c1], ((0, 0), (0, LANES - 4 * N_HEADS_M))),
        w_qt=wb[:, c1:c2].T,
        w_kp=jnp.pad(wk, ((0, 0), (0, 0), (0, LANES - DH_A))).reshape(D_MODEL, N_KV_A * LANES),
        w_vt=wb[:, c3:c4].T,
        w_z=wb[:, c4:],
        bias=jnp.transpose(b_if.astype(F32), (2, 0, 1)).reshape(N_HEADS_M, 4),
        g_pre=norm_pre.astype(F32).reshape(1, D_MODEL),
        g_fin=norm_final.astype(F32).reshape(1, D_MODEL),
        ones=jnp.asarray(blk[:, None] == blk[None, :], dtype=BF16),
        w_mo=w_mo.astype(BF16), w_ao=w_ao.astype(BF16), w_out=w_out.astype(BF16),
    )


def _tiles(seq):
    return min(512, seq), min(2048, seq), min(256, seq), min(512, seq)


def _layer(x, p, q_gain, k_gain):
    bsz, seq, _ = x.shape
    tm, sb, tq, tk = _tiles(seq)
    nc = seq // CHUNK
    x2 = x.reshape(bsz * seq, D_MODEL)

    q_m, k_m, v_m, gate_m, if_raw = _mproj(x2, p["g_pre"], p["w_m"], p["w_if"], tm)
    qt, kc, vtc = _aproj(x, p["g_pre"], p["w_qt"], p["w_kp"], p["w_vt"], p["ones"],
                         *_rope_tables(seq, q_gain, k_gain), tk)

    shp = (bsz, seq, W_M)
    gates = if_raw[:, :4 * N_HEADS_M].reshape(bsz, nc, CHUNK, 4, N_HEADS_M)
    gates = jnp.transpose(gates, (0, 4, 3, 1, 2))
    kt = jnp.transpose(k_m.reshape(bsz, nc, CHUNK, N_HEADS_M, DH_M), (0, 3, 1, 4, 2))
    h_f, h_b = _mlstm(p["bias"], q_m.reshape(shp), k_m.reshape(shp), v_m.reshape(shp), kt, gates, sb)

    att = _attn(qt, kc, vtc, jnp.eye(tq, dtype=BF16), tq)

    y = _final(x2, h_f.reshape(-1, W_M), h_b.reshape(-1, W_M), gate_m, att.reshape(-1, W_A),
               p["g_pre"], p["g_fin"], p["w_z"], p["w_mo"], p["w_ao"], p["w_out"], min(256, seq))
    return y.reshape(bsz, seq, D_MODEL)


def kernel(x_prompt, x_sample, w_in, b_if, norm_pre, q_gain, k_gain, w_mo, w_ao, w_out, norm_final):
    assert w_in.shape[0] == 1, "single-layer model"
    p = _prep_weights(w_in[0], b_if[0], norm_pre[0], w_mo[0], w_ao[0], w_out[0], norm_final)
    return (_layer(x_prompt, p, q_gain[0], k_gain[0]), _layer(x_sample, p, q_gain[0], k_gain[0]))
```

```python
import functools

import numpy as np
import jax
import jax.numpy as jnp
from jax import lax
from jax.experimental import pallas as pl
from jax.experimental.pallas import tpu as pltpu

F32 = jnp.float32
BF16 = jnp.bfloat16

D_MODEL = 1024
N_HEADS_M = 4
DH_M = 256
W_M = N_HEADS_M * DH_M
CHUNK = 64
N_HEADS_A = 16
N_KV_A = 4
GROUP_A = N_HEADS_A // N_KV_A
DH_A = 64
W_A = N_HEADS_A * DH_A
W_KV = N_KV_A * DH_A
GRID_W = 64
AXIS_DIM = DH_A // 2
ROPE_THETA = 10000.0
EPS = 1e-6

LANES = 128
VT_ROWS = 80
MXU_DIM = 256
ATT_CHUNK = 128
ATT_TICKS = 4
LOG2E = 1.4426950408889634
PAIR = 2 * CHUNK
MLSTM_UNROLL = 2
VMEM_LIMIT = 56 * 1024 * 1024


def _cparams(sem):
    return pltpu.CompilerParams(dimension_semantics=sem, vmem_limit_bytes=VMEM_LIMIT)


def _rms_bf16(x, g):
    ms = jnp.mean(x * x, axis=-1, keepdims=True)
    return (x * lax.rsqrt(ms + EPS) * g).astype(BF16)


def _dot(a, b):
    return jnp.dot(a, b, preferred_element_type=F32)


def _dot_nt(a, b):
    return lax.dot_general(a, b, (((1,), (1,)), ((), ())), preferred_element_type=F32)


def _mproj_kernel(x_ref, g_ref, w_ref, wkt_ref, wif_ref, q_ref, kt_ref, v_ref, gate_ref, if_ref):
    tm = x_ref.shape[0]
    h = _rms_bf16(x_ref[...], g_ref[...])

    def proj(c):
        return _dot(h, w_ref[:, c * W_M:(c + 1) * W_M])

    q_ref[...] = proj(0).astype(BF16)
    v_ref[...] = proj(1).astype(BF16)
    o = proj(2)
    z = proj(3)
    gate_ref[...] = (jax.nn.sigmoid(o) * (z * jax.nn.sigmoid(z))).astype(BF16)
    if_ref[...] = _dot(h, wif_ref[...])
    kt = _dot_nt(wkt_ref[...], h) * (DH_M ** -0.5)
    for hh in range(N_HEADS_M):
        for p in range(tm // PAIR):
            kt_ref[hh, p] = kt[hh * DH_M:(hh + 1) * DH_M, p * PAIR:(p + 1) * PAIR].astype(BF16)


def _mproj(x2, g, w, wkt, wif, tm):
    t = x2.shape[0]
    tok = lambda i: (i, 0)
    fixed = lambda i: (0, 0)
    big = jax.ShapeDtypeStruct((t, W_M), BF16)
    return pl.pallas_call(
        _mproj_kernel,
        grid=(t // tm,),
        in_specs=[pl.BlockSpec((tm, D_MODEL), tok), pl.BlockSpec((1, D_MODEL), fixed),
                  pl.BlockSpec((D_MODEL, 4 * W_M), fixed), pl.BlockSpec((W_M, D_MODEL), fixed),
                  pl.BlockSpec((D_MODEL, LANES), fixed)],
        out_specs=[pl.BlockSpec((tm, W_M), tok),
                   pl.BlockSpec((N_HEADS_M, tm // PAIR, DH_M, PAIR), lambda i: (0, i, 0, 0)),
                   pl.BlockSpec((tm, W_M), tok), pl.BlockSpec((tm, W_M), tok), pl.BlockSpec((tm, LANES), tok)],
        out_shape=[big, jax.ShapeDtypeStruct((N_HEADS_M, t // PAIR, DH_M, PAIR), BF16), big, big,
                   jax.ShapeDtypeStruct((t, LANES), F32)],
        compiler_params=_cparams(("parallel",)),
        name="mproj",
    )(x2, g, w, wkt, wif)


def _aproj_kernel(x_ref, g_ref, wqt_ref, wkp_ref, wvt_ref, ones_ref, qa_ref, qb_ref, ka_ref, kup_ref, kdn_ref,
                  qt_ref, k_ref, vt_ref):
    tm = x_ref.shape[0]
    h = _rms_bf16(x_ref[...], g_ref[...])
    half = AXIS_DIM // 2

    qt = _dot_nt(wqt_ref[...], h).reshape(N_HEADS_A, DH_A, tm)
    xh = qt * lax.rsqrt(jnp.mean(qt * qt, axis=1, keepdims=True) + EPS)
    x5 = xh.reshape(N_HEADS_A, 2, 2, half, tm)
    partner = jnp.concatenate([x5[:, :, 1:2], x5[:, :, 0:1]], axis=2).reshape(N_HEADS_A, DH_A, tm)
    qt_ref[...] = (xh * qa_ref[...] + partner * qb_ref[...]).astype(BF16)

    kp = _dot(h, wkp_ref[...])
    ones = ones_ref[...]
    ka, kup, kdn = ka_ref[...], kup_ref[...], kdn_ref[...]
    for c in range(N_KV_A // 2):
        blk = kp[:, c * 256:(c + 1) * 256]
        ss = _dot((blk * blk).astype(BF16), ones)
        y = blk * lax.rsqrt(ss * (1.0 / DH_A) + EPS)
        for j in range(2):
            yc = y[:, j * LANES:(j + 1) * LANES]
            up = pltpu.roll(yc, LANES - half, 1)
            dn = pltpu.roll(yc, half, 1)
            k_ref[2 * c + j] = (yc * ka + up * kup + dn * kdn)[:, :DH_A].astype(BF16)

    vt = _dot_nt(wvt_ref[...], h)
    row = lax.broadcasted_iota(jnp.int32, (VT_ROWS - DH_A, tm), 0)
    extra = jnp.where(row == 0, 1.0, 0.0).astype(BF16)
    for hh in range(N_KV_A):
        vt_ref[hh, :DH_A, :] = vt[hh * DH_A:(hh + 1) * DH_A, :].astype(BF16)
        vt_ref[hh, DH_A:, :] = extra


def _aproj(x3, g, wqt, wkp, wvt, ones, qa, qb, ka, kup, kdn, tk):
    bsz, seq, _ = x3.shape
    nk = seq // tk
    fixed = lambda b, j: (0, 0)
    return pl.pallas_call(
        _aproj_kernel,
        grid=(bsz, nk),
        in_specs=[pl.BlockSpec((None, tk, D_MODEL), lambda b, j: (b, j, 0)), pl.BlockSpec((1, D_MODEL), fixed),
                  pl.BlockSpec((W_A, D_MODEL), fixed), pl.BlockSpec((D_MODEL, N_KV_A * LANES), fixed),
                  pl.BlockSpec((W_KV, D_MODEL), fixed), pl.BlockSpec((256, 256), fixed),
                  pl.BlockSpec((DH_A, tk), lambda b, j: (0, j)), pl.BlockSpec((DH_A, tk), lambda b, j: (0, j)),
                  pl.BlockSpec((tk, LANES), lambda b, j: (j, 0)), pl.BlockSpec((tk, LANES), lambda b, j: (j, 0)),
                  pl.BlockSpec((tk, LANES), lambda b, j: (j, 0))],
        out_specs=[pl.BlockSpec((None, N_HEADS_A, DH_A, tk), lambda b, j: (b, 0, 0, j)),
                   pl.BlockSpec((None, N_KV_A, None, tk, DH_A), lambda b, j: (b, 0, j, 0, 0)),
                   pl.BlockSpec((None, N_KV_A, None, VT_ROWS, tk), lambda b, j: (b, 0, j, 0, 0))],
        out_shape=[jax.ShapeDtypeStruct((bsz, N_HEADS_A, DH_A, seq), BF16),
                   jax.ShapeDtypeStruct((bsz, N_KV_A, nk, tk, DH_A), BF16),
                   jax.ShapeDtypeStruct((bsz, N_KV_A, nk, VT_ROWS, tk), BF16)],
        compiler_params=_cparams(("parallel", "parallel")),
        name="aproj",
    )(x3, g, wqt, wkp, wvt, ones, qa, qb, ka, kup, kdn)


def _split3(x):
    h1 = x.astype(BF16)
    r1 = x - h1.astype(F32)
    h2 = r1.astype(BF16)
    return h1, h2, (r1 - h2.astype(F32)).astype(BF16)


def _mlstm_kernel(bias_ref,
                  qf_ref, vf_ref, ktf_ref, gf_ref,
                  qb_ref, vb_ref, ktb_ref, gb_ref,
                  hf_ref, hb_ref,
                  cta_scr, m_scr, a_scr, lf_scr, mc_scr, *, npb):
    head = pl.program_id(1)

    @pl.when(pl.program_id(2) == 0)
    def _():
        cta_scr[...] = jnp.zeros_like(cta_scr)
        m_scr[...] = jnp.zeros_like(m_scr)

    ti = lax.broadcasted_iota(jnp.int32, (CHUNK, PAIR), 0)
    li = lax.broadcasted_iota(jnp.int32, (CHUNK, PAIR), 1)
    ri = lax.broadcasted_iota(jnp.int32, (PAIR, PAIR), 0)
    ci = lax.broadcasted_iota(jnp.int32, (PAIR, PAIR), 1)
    same_chunk = (ri >= CHUNK) == (ci >= CHUNK)
    low_half = lax.broadcasted_iota(jnp.int32, (npb, PAIR), 1) < CHUNK
    ones_b = jnp.ones((PAIR, LANES), BF16)

    def valid_mask(d, par):
        rel = li - CHUNK * par
        return (rel >= ti) & (rel < CHUNK) if d else (rel <= ti) & (rel >= 0)

    for d, g_ref in ((0, gf_ref), (1, gb_ref)):
        last = 0 if d else CHUNK - 1
        xf = g_ref[2 * d + 1] + bias_ref[head, 2 * d + 1]
        lf = (jnp.minimum(xf, 0.0) - jnp.log(1.0 + jnp.exp(-jnp.abs(xf)))) * LOG2E
        cmat = jnp.where(same_chunk & ((ri >= ci) if d else (ri <= ci)), 1.0, 0.0).astype(BF16)
        b = sum(_dot(t, cmat) for t in _split3(lf))
        a = (g_ref[2 * d] + bias_ref[head, 2 * d]) * LOG2E - b
        a_scr[d] = a
        lf_scr[d] = lf
        amax = [jnp.broadcast_to(jnp.max(jnp.where(low_half == (par == 0), a, -jnp.inf), axis=1, keepdims=True),
                                 (npb, LANES)) for par in (0, 1)]
        blast = [jnp.broadcast_to(b[:, CHUNK * par + last:CHUNK * par + last + 1], (npb, LANES)) for par in (0, 1)]
        m = m_scr[d:d + 1, :]
        for p in (range(npb - 1, -1, -1) if d else range(npb)):
            for par in ((1, 0) if d else (0, 1)):
                c = 2 * p + par
                mc_scr[d, c:c + 1, :] = m
                m = blast[par][p:p + 1, :] + jnp.maximum(m, amax[par][p:p + 1, :])
        m_scr[d:d + 1, :] = m

    dirs = ((qf_ref, vf_ref, ktf_ref, hf_ref), (qb_ref, vb_ref, ktb_ref, hb_ref))

    def rows_of(c):
        return pl.ds(pl.multiple_of(c * CHUNK, CHUNK), CHUNK)

    def stage1(d, p, par):
        q_ref, _, kt_ref, _ = dirs[d]
        c = 2 * p + par
        valid = valid_mask(d, par)
        last = 0 if d else CHUNK - 1
        a_row = a_scr[d, pl.ds(p, 1), :]
        lf_row = lf_scr[d, pl.ds(p, 1), :]
        m = mc_scr[d, pl.ds(c, 1), 0:1]
        cm = jnp.max(jnp.where(valid, a_row, -jnp.inf), axis=1, keepdims=True)
        bcol = jnp.sum(jnp.where(valid, lf_row, 0.0), axis=1, keepdims=True)
        mc = jnp.maximum(cm, m)
        dmat = jnp.where(valid, jnp.exp2(a_row - mc), 0.0)
        inter = jnp.broadcast_to(jnp.exp2(m - mc), (CHUNK, LANES))
        emt = jnp.broadcast_to(jnp.exp2(-(bcol + mc)), (CHUNK, LANES))
        dec = jnp.broadcast_to(jnp.exp2(m - mc[last:last + 1]), (1, LANES))
        sd = (_dot(q_ref[rows_of(c), :], kt_ref[p]) * dmat).astype(BF16)
        return sd, dmat[last:last + 1, :], inter, emt, dec

    def stage2(d, p, par, pre):
        sd, ws_row, inter, emt, dec = pre
        q_ref, v_ref, kt_ref, h_ref = dirs[d]
        rows = rows_of(2 * p + par)
        q = q_ref[rows, :]
        v_pair = v_ref[pl.ds(pl.multiple_of(p * PAIR, PAIR), PAIR), :]
        v_aug = jnp.concatenate([v_pair, ones_b], axis=1)
        cta = cta_scr[d]
        ktw = (kt_ref[p].astype(F32) * ws_row).astype(BF16)
        both = _dot(jnp.concatenate([sd, ktw], axis=0), v_aug)
        na = both[:CHUNK] + jnp.concatenate([inter, inter, inter], axis=1) * _dot(q, cta.astype(BF16))
        r = 1.0 / jnp.maximum(jnp.abs(na[:, DH_M:]), emt)
        h_ref[rows, :] = (na[:, :DH_M] * jnp.concatenate([r, r], axis=1)).astype(BF16)
        cta_scr[d] = jnp.concatenate([dec, dec, dec], axis=1) * cta + both[CHUNK:]

    def body(i, carry):
        pf = i
        pb = npb - 1 - i
        mid_f = stage1(0, pf, 1)
        mid_b = stage1(1, pb, 0)
        stage2(0, pf, 0, carry[0])
        stage2(1, pb, 1, carry[1])
        nxt_f = stage1(0, jnp.minimum(pf + 1, npb - 1), 0)
        nxt_b = stage1(1, jnp.maximum(pb - 1, 0), 1)
        stage2(0, pf, 1, mid_f)
        stage2(1, pb, 0, mid_b)
        return nxt_f, nxt_b

    lax.fori_loop(0, npb, body, (stage1(0, 0, 0), stage1(1, npb - 1, 1)), unroll=MLSTM_UNROLL)


def _mlstm(bias, q, v, kt, gates, sb):
    bsz, seq, _ = q.shape
    nb = seq // sb
    npb = sb // PAIR
    fwd = lambda b, h, j: (b, j, h)
    bwd = lambda b, h, j: (b, nb - 1 - j, h)
    kfwd = lambda b, h, j: (h, b * nb + j, 0, 0)
    kbwd = lambda b, h, j: (h, b * nb + nb - 1 - j, 0, 0)
    gfwd = lambda b, h, j: (b, h, 0, j, 0)
    gbwd = lambda b, h, j: (b, h, 0, nb - 1 - j, 0)
    tok = lambda im: pl.BlockSpec((None, sb, DH_M), im)
    ktb = lambda im: pl.BlockSpec((None, npb, DH_M, PAIR), im)
    gb = lambda im: pl.BlockSpec((None, None, 4, npb, PAIR), im)
    out = jax.ShapeDtypeStruct((bsz, seq, W_M), BF16)
    return pl.pallas_call(
        functools.partial(_mlstm_kernel, npb=npb),
        grid=(bsz, N_HEADS_M, nb),
        in_specs=[pl.BlockSpec(memory_space=pltpu.SMEM),
                  tok(fwd), tok(fwd), ktb(kfwd), gb(gfwd),
                  tok(bwd), tok(bwd), ktb(kbwd), gb(gbwd)],
        out_specs=[tok(fwd), tok(bwd)],
        out_shape=[out, out],
        scratch_shapes=[pltpu.VMEM((2, DH_M, DH_M + LANES), F32), pltpu.VMEM((8, LANES), F32),
                        pltpu.VMEM((2, npb, PAIR), F32), pltpu.VMEM((2, npb, PAIR), F32),
                        pltpu.VMEM((2, 2 * npb, LANES), F32)],
        compiler_params=_cparams(("parallel", "parallel", "arbitrary")),
        name="mlstm",
    )(bias, q, v, kt, gates, q, v, kt, gates)


def _attn_kernel(qt_ref, k_ref, vt_ref, eye_ref, o_ref, st0, st1, *, nk):
    st_scr = (st0, st1)
    tq = qt_ref.shape[-1]
    tk = k_ref.shape[1]
    qs = [qt_ref[g] for g in range(GROUP_A)]

    def tick(t, slot, do_qk, do_sm, state):
        pmax, ms, accs = state
        n_pmax, n_ms, n_accs = list(pmax), list(ms), list(accs)
        for g in range(GROUP_A):
            if do_sm:
                m_new = jnp.maximum(ms[g], jnp.max(pmax[g], axis=0, keepdims=True))
                acc = jnp.exp2(ms[g] - m_new) * accs[g]
                n_ms[g] = m_new
            pm = None
            pts = []
            for c in range(tk // ATT_CHUNK):
                rows = slice(c * ATT_CHUNK, (c + 1) * ATT_CHUNK)
                if do_qk:
                    st = _dot(k_ref[t, rows, :], qs[g])
                    st_scr[slot][g, rows, :] = st
                    cm = jnp.max(st.reshape(ATT_CHUNK // 8, 8, tq), axis=0)
                    pm = cm if pm is None else jnp.maximum(pm, cm)
                if do_sm:
                    pts.append(jnp.exp2(st_scr[1 - slot][g, rows, :] - m_new).astype(BF16))
                    if (c + 1) * ATT_CHUNK % MXU_DIM == 0:
                        kt = (c + 1) * ATT_CHUNK // MXU_DIM - 1
                        cols = slice(kt * MXU_DIM, (kt + 1) * MXU_DIM)
                        acc = acc + _dot(vt_ref[t - 1, :, cols], jnp.concatenate(pts, axis=0))
                        pts = []
            if do_qk:
                n_pmax[g] = pm
            if do_sm:
                n_accs[g] = acc
        return tuple(n_pmax), tuple(n_ms), tuple(n_accs)

    neg = tuple(jnp.full((1, tq), -jnp.inf, F32) for _ in range(GROUP_A))
    neg8 = tuple(jnp.full((8, tq), -jnp.inf, F32) for _ in range(GROUP_A))
    state = (neg8, neg, tuple(jnp.zeros((VT_ROWS, tq), F32) for _ in range(GROUP_A)))
    state = tick(0, 0, True, False, state)
    state = tick(1, 1, True, True, state)
    ntrip = (nk - 2) // ATT_TICKS

    def body(i, state):
        for u in range(ATT_TICKS):
            state = tick(ATT_TICKS * i + 2 + u, u % 2, True, True, state)
        return state

    state = lax.fori_loop(0, ntrip, body, state)
    for t in range(2 + ntrip * ATT_TICKS, nk):
        state = tick(t, t % 2, True, True, state)
    state = tick(nk, nk % 2, False, True, state)
    ot = jnp.concatenate(
        [(acc[:DH_A] * (1.0 / acc[DH_A:DH_A + 1])).astype(BF16) for acc in state[2]], axis=0)
    o_ref[...] = _dot_nt(eye_ref[...], ot).astype(BF16)


def _attn(qt, kc, vtc, eye, tq):
    bsz, _, _, seq = qt.shape
    nk, tk = kc.shape[2], kc.shape[3]
    assert nk >= 2 and tk % MXU_DIM == 0
    return pl.pallas_call(
        functools.partial(_attn_kernel, nk=nk),
        grid=(bsz, N_KV_A, seq // tq),
        in_specs=[pl.BlockSpec((None, GROUP_A, DH_A, tq), lambda b, h, i: (b, h, 0, i)),
                  pl.BlockSpec((None, None, nk, tk, DH_A), lambda b, h, i: (b, h, 0, 0, 0)),
                  pl.BlockSpec((None, None, nk, VT_ROWS, tk), lambda b, h, i: (b, h, 0, 0, 0)),
                  pl.BlockSpec((tq, tq), lambda b, h, i: (0, 0))],
        out_specs=pl.BlockSpec((None, tq, GROUP_A * DH_A), lambda b, h, i: (b, i, h)),
        out_shape=jax.ShapeDtypeStruct((bsz, seq, W_A), BF16),
        scratch_shapes=[pltpu.VMEM((GROUP_A, tk, tq), F32), pltpu.VMEM((GROUP_A, tk, tq), F32)],
        compiler_params=_cparams(("parallel", "parallel", "arbitrary")),
        name="attn",
    )(qt, kc, vtc, eye)


def _final_kernel(x_ref, hf_ref, hb_ref, gm_ref, att_ref, gpre_ref, gfin_ref,
                  wz_ref, wmo_ref, wao_ref, wout_ref, y_ref):
    x = x_ref[...]
    h = _rms_bf16(x, gpre_ref[...])
    z_a = _dot(h, wz_ref[:, :W_A])
    um = ((hf_ref[...].astype(F32) + hb_ref[...].astype(F32)) * gm_ref[...].astype(F32)).astype(BF16)
    ua = (att_ref[...].astype(F32) * (z_a * jax.nn.sigmoid(z_a))).astype(BF16)
    y_m = _dot(um, wmo_ref[...])
    y_a = _dot(ua, wao_ref[...])
    g_m = _dot(h, wz_ref[:, W_A:W_A + D_MODEL])
    g_a = _dot(h, wz_ref[:, W_A + D_MODEL:])
    merged = jax.nn.sigmoid(g_m) * y_m + jax.nn.sigmoid(g_a) * y_a
    out = x + _dot(merged.astype(BF16), wout_ref[...])
    ms = jnp.mean(out * out, axis=-1, keepdims=True)
    y_ref[...] = out * lax.rsqrt(ms + EPS) * gfin_ref[...]


def _final(x2, hf, hb, gm, att, gpre, gfin, wz, wmo, wao, wout, tm):
    t = x2.shape[0]
    tok = lambda i: (i, 0)
    fixed = lambda i: (0, 0)
    act = pl.BlockSpec((tm, D_MODEL), tok)
    sq = pl.BlockSpec((D_MODEL, D_MODEL), fixed)
    vec = pl.BlockSpec((1, D_MODEL), fixed)
    return pl.pallas_call(
        _final_kernel,
        grid=(t // tm,),
        in_specs=[act, act, act, act, act, vec, vec, pl.BlockSpec((D_MODEL, 3 * D_MODEL), fixed), sq, sq, sq],
        out_specs=act,
        out_shape=jax.ShapeDtypeStruct((t, D_MODEL), F32),
        compiler_params=_cparams(("parallel",)),
        name="final",
    )(x2, hf, hb, gm, att, gpre, gfin, wz, wmo, wao, wout)


def _rope_tables(seq, q_gain, k_gain):
    half = AXIS_DIM // 2
    d = np.arange(DH_A)
    axis = d // AXIS_DIM
    lo = (d % AXIS_DIM) < half
    idx = (d % AXIS_DIM) % half
    partner = np.where(lo, d + half, d - half)
    freqs = ROPE_THETA ** (-jnp.arange(0, AXIS_DIM, 2, dtype=F32) / AXIS_DIM)
    s = jnp.arange(seq)
    pos = jnp.where(jnp.asarray(axis)[:, None] == 0, (s // GRID_W)[None, :], (s % GRID_W)[None, :])
    ang = pos.astype(F32) * freqs[jnp.asarray(idx)][:, None]
    cos, sin = jnp.cos(ang), jnp.sin(ang)
    sign = jnp.where(jnp.asarray(lo), -1.0, 1.0)[:, None]

    def tables(gain, scale):
        g = gain.astype(F32) * scale
        return g[:, None] * cos, sign * g[jnp.asarray(partner)][:, None] * sin

    qa, qb = tables(q_gain, LOG2E * DH_A ** -0.5)
    ka, kb = tables(k_gain, 1.0)
    pad = lambda t: jnp.pad(t.T, ((0, 0), (0, LANES - DH_A)))
    lo_row = jnp.asarray(lo)[:, None]
    return qa, qb, pad(ka), pad(jnp.where(lo_row, kb, 0.0)), pad(jnp.where(lo_row, 0.0, kb))


def _prep_weights(w_in, b_if, norm_pre, w_mo, w_ao, w_out, norm_final):
    c0 = 5 * W_M
    c1 = c0 + 4 * N_HEADS_M
    c2 = c1 + W_A
    c3 = c2 + W_KV
    c4 = c3 + W_KV
    wb = w_in.astype(BF16)
    blk = np.arange(256) // LANES
    wk = wb[:, c2:c3].reshape(D_MODEL, N_KV_A, DH_A)
    return dict(
        w_m=jnp.concatenate([wb[:, :W_M], wb[:, 2 * W_M:c0]], axis=1),
        w_kt=wb[:, W_M:2 * W_M].T,
        w_if=jnp.pad(wb[:, c0:c1], ((0, 0), (0, LANES - 4 * N_HEADS_M))),
        w_qt=wb[:, c1:c2].T,
        w_kp=jnp.pad(wk, ((0, 0), (0, 0), (0, LANES - DH_A))).reshape(D_MODEL, N_KV_A * LANES),
        w_vt=wb[:, c3:c4].T,
        w_z=wb[:, c4:],
        bias=jnp.transpose(b_if.astype(F32), (2, 0, 1)).reshape(N_HEADS_M, 4),
        g_pre=norm_pre.astype(F32).reshape(1, D_MODEL),
        g_fin=norm_final.astype(F32).reshape(1, D_MODEL),
        ones=jnp.asarray(blk[:, None] == blk[None, :], dtype=BF16),
        w_mo=w_mo.astype(BF16), w_ao=w_ao.astype(BF16), w_out=w_out.astype(BF16),
    )


def _tiles(seq):
    return min(512, seq), min(2048, seq), min(512, seq), min(512, seq)


def _layer(x, p, q_gain, k_gain):
    bsz, seq, _ = x.shape
    tm, sb, tq, tk = _tiles(seq)
    x2 = x.reshape(bsz * seq, D_MODEL)

    q_m, kt, v_m, gate_m, if_raw = _mproj(x2, p["g_pre"], p["w_m"], p["w_kt"], p["w_if"], tm)
    qt, kc, vtc = _aproj(x, p["g_pre"], p["w_qt"], p["w_kp"], p["w_vt"], p["ones"],
                         *_rope_tables(seq, q_gain, k_gain), tk)

    shp = (bsz, seq, W_M)
    gates = if_raw[:, :4 * N_HEADS_M].reshape(bsz, seq // PAIR, PAIR, 4, N_HEADS_M)
    gates = jnp.transpose(gates, (0, 4, 3, 1, 2))
    h_f, h_b = _mlstm(p["bias"], q_m.reshape(shp), v_m.reshape(shp), kt, gates, sb)

    att = _attn(qt, kc, vtc, jnp.eye(tq, dtype=BF16), tq)

    y = _final(x2, h_f.reshape(-1, W_M), h_b.reshape(-1, W_M), gate_m, att.reshape(-1, W_A),
               p["g_pre"], p["g_fin"], p["w_z"], p["w_mo"], p["w_ao"], p["w_out"], min(256, seq))
    return y.reshape(bsz, seq, D_MODEL)


def kernel(x_prompt, x_sample, w_in, b_if, norm_pre, q_gain, k_gain, w_mo, w_ao, w_out, norm_final):
    assert w_in.shape[0] == 1, "single-layer model"
    p = _prep_weights(w_in[0], b_if[0], norm_pre[0], w_mo[0], w_ao[0], w_out[0], norm_final)
    return (_layer(x_prompt, p, q_gain[0], k_gain[0]), _layer(x_sample, p, q_gain[0], k_gain[0]))
```

```python
import functools

import numpy as np
import jax
import jax.numpy as jnp
from jax import lax
from jax.experimental import pallas as pl
from jax.experimental.pallas import tpu as pltpu

F32 = jnp.float32
BF16 = jnp.bfloat16

D_MODEL = 1024
N_HEADS_M = 4
DH_M = 256
W_M = N_HEADS_M * DH_M
CHUNK = 64
N_HEADS_A = 16
N_KV_A = 4
GROUP_A = N_HEADS_A // N_KV_A
DH_A = 64
W_A = N_HEADS_A * DH_A
W_KV = N_KV_A * DH_A
GRID_W = 64
AXIS_DIM = DH_A // 2
ROPE_THETA = 10000.0
EPS = 1e-6

LANES = 128
VT_ROWS = 80
MXU_DIM = 256
ATT_CHUNK = 128
ATT_TICKS = 4
LOG2E = 1.4426950408889634
PAIR = 2 * CHUNK
MLSTM_UNROLL = 2
VMEM_LIMIT = 56 * 1024 * 1024


def _cparams(sem):
    return pltpu.CompilerParams(dimension_semantics=sem, vmem_limit_bytes=VMEM_LIMIT)


def _rms_bf16(x, g):
    ms = jnp.mean(x * x, axis=-1, keepdims=True)
    return (x * lax.rsqrt(ms + EPS) * g).astype(BF16)


def _dot(a, b):
    return jnp.dot(a, b, preferred_element_type=F32)


def _dot_nt(a, b):
    return lax.dot_general(a, b, (((1,), (1,)), ((), ())), preferred_element_type=F32)


def _mproj_kernel(x_ref, g_ref, w_ref, wkt_ref, wif_ref, q_ref, kt_ref, v_ref, gate_ref, if_ref):
    tm = x_ref.shape[0]
    h = _rms_bf16(x_ref[...], g_ref[...])

    def proj(c):
        return _dot(h, w_ref[:, c * W_M:(c + 1) * W_M])

    q_ref[...] = proj(0).astype(BF16)
    v_ref[...] = proj(1).astype(BF16)
    o = proj(2)
    z = proj(3)
    gate_ref[...] = (jax.nn.sigmoid(o) * (z * jax.nn.sigmoid(z))).astype(BF16)
    if_ref[...] = _dot(h, wif_ref[...])
    kt = _dot_nt(wkt_ref[...], h) * (DH_M ** -0.5)
    for hh in range(N_HEADS_M):
        for p in range(tm // PAIR):
            kt_ref[hh, p] = kt[hh * DH_M:(hh + 1) * DH_M, p * PAIR:(p + 1) * PAIR].astype(BF16)


def _mproj(x2, g, w, wkt, wif, tm):
    t = x2.shape[0]
    tok = lambda i: (i, 0)
    fixed = lambda i: (0, 0)
    big = jax.ShapeDtypeStruct((t, W_M), BF16)
    return pl.pallas_call(
        _mproj_kernel,
        grid=(t // tm,),
        in_specs=[pl.BlockSpec((tm, D_MODEL), tok), pl.BlockSpec((1, D_MODEL), fixed),
                  pl.BlockSpec((D_MODEL, 4 * W_M), fixed), pl.BlockSpec((W_M, D_MODEL), fixed),
                  pl.BlockSpec((D_MODEL, LANES), fixed)],
        out_specs=[pl.BlockSpec((tm, W_M), tok),
                   pl.BlockSpec((N_HEADS_M, tm // PAIR, DH_M, PAIR), lambda i: (0, i, 0, 0)),
                   pl.BlockSpec((tm, W_M), tok), pl.BlockSpec((tm, W_M), tok), pl.BlockSpec((tm, LANES), tok)],
        out_shape=[big, jax.ShapeDtypeStruct((N_HEADS_M, t // PAIR, DH_M, PAIR), BF16), big, big,
                   jax.ShapeDtypeStruct((t, LANES), F32)],
        compiler_params=_cparams(("parallel",)),
        name="mproj",
    )(x2, g, w, wkt, wif)


def _aproj_kernel(x_ref, g_ref, wqt_ref, wkp_ref, wvt_ref, ones_ref, qa_ref, qb_ref, ka_ref, kup_ref, kdn_ref,
                  qt_ref, k_ref, vt_ref):
    tm = x_ref.shape[0]
    h = _rms_bf16(x_ref[...], g_ref[...])
    half = AXIS_DIM // 2

    qt = _dot_nt(wqt_ref[...], h).reshape(N_HEADS_A, DH_A, tm)
    xh = qt * lax.rsqrt(jnp.mean(qt * qt, axis=1, keepdims=True) + EPS)
    x5 = xh.reshape(N_HEADS_A, 2, 2, half, tm)
    partner = jnp.concatenate([x5[:, :, 1:2], x5[:, :, 0:1]], axis=2).reshape(N_HEADS_A, DH_A, tm)
    qt_ref[...] = (xh * qa_ref[...] + partner * qb_ref[...]).astype(BF16)

    kp = _dot(h, wkp_ref[...])
    ones = ones_ref[...]
    ka, kup, kdn = ka_ref[...], kup_ref[...], kdn_ref[...]
    for c in range(N_KV_A // 2):
        blk = kp[:, c * 256:(c + 1) * 256]
        ss = _dot((blk * blk).astype(BF16), ones)
        y = blk * lax.rsqrt(ss * (1.0 / DH_A) + EPS)
        for j in range(2):
            yc = y[:, j * LANES:(j + 1) * LANES]
            up = pltpu.roll(yc, LANES - half, 1)
            dn = pltpu.roll(yc, half, 1)
            k_ref[2 * c + j] = (yc * ka + up * kup + dn * kdn)[:, :DH_A].astype(BF16)

    vt = _dot_nt(wvt_ref[...], h)
    row = lax.broadcasted_iota(jnp.int32, (VT_ROWS - DH_A, tm), 0)
    extra = jnp.where(row == 0, 1.0, 0.0).astype(BF16)
    for hh in range(N_KV_A):
        vt_ref[hh, :DH_A, :] = vt[hh * DH_A:(hh + 1) * DH_A, :].astype(BF16)
        vt_ref[hh, DH_A:, :] = extra


def _aproj(x3, g, wqt, wkp, wvt, ones, qa, qb, ka, kup, kdn, tk):
    bsz, seq, _ = x3.shape
    nk = seq // tk
    fixed = lambda b, j: (0, 0)
    return pl.pallas_call(
        _aproj_kernel,
        grid=(bsz, nk),
        in_specs=[pl.BlockSpec((None, tk, D_MODEL), lambda b, j: (b, j, 0)), pl.BlockSpec((1, D_MODEL), fixed),
                  pl.BlockSpec((W_A, D_MODEL), fixed), pl.BlockSpec((D_MODEL, N_KV_A * LANES), fixed),
                  pl.BlockSpec((W_KV, D_MODEL), fixed), pl.BlockSpec((256, 256), fixed),
                  pl.BlockSpec((DH_A, tk), lambda b, j: (0, j)), pl.BlockSpec((DH_A, tk), lambda b, j: (0, j)),
                  pl.BlockSpec((tk, LANES), lambda b, j: (j, 0)), pl.BlockSpec((tk, LANES), lambda b, j: (j, 0)),
                  pl.BlockSpec((tk, LANES), lambda b, j: (j, 0))],
        out_specs=[pl.BlockSpec((None, N_HEADS_A, DH_A, tk), lambda b, j: (b, 0, 0, j)),
                   pl.BlockSpec((None, N_KV_A, None, tk, DH_A), lambda b, j: (b, 0, j, 0, 0)),
                   pl.BlockSpec((None, N_KV_A, None, VT_ROWS, tk), lambda b, j: (b, 0, j, 0, 0))],
        out_shape=[jax.ShapeDtypeStruct((bsz, N_HEADS_A, DH_A, seq), BF16),
                   jax.ShapeDtypeStruct((bsz, N_KV_A, nk, tk, DH_A), BF16),
                   jax.ShapeDtypeStruct((bsz, N_KV_A, nk, VT_ROWS, tk), BF16)],
        compiler_params=_cparams(("parallel", "parallel")),
        name="aproj",
    )(x3, g, wqt, wkp, wvt, ones, qa, qb, ka, kup, kdn)


def _split3(x):
    h1 = x.astype(BF16)
    r1 = x - h1.astype(F32)
    h2 = r1.astype(BF16)
    return h1, h2, (r1 - h2.astype(F32)).astype(BF16)


def _mlstm_kernel(bias_ref,
                  qf_ref, vf_ref, ktf_ref, gf_ref,
                  qb_ref, vb_ref, ktb_ref, gb_ref,
                  hf_ref, hb_ref,
                  cta_scr, m_scr, a_scr, lf_scr, mc_scr, *, npb):
    head = pl.program_id(1)

    @pl.when(pl.program_id(2) == 0)
    def _():
        cta_scr[...] = jnp.zeros_like(cta_scr)
        m_scr[...] = jnp.zeros_like(m_scr)

    ti = lax.broadcasted_iota(jnp.int32, (CHUNK, PAIR), 0)
    li = lax.broadcasted_iota(jnp.int32, (CHUNK, PAIR), 1)
    ri = lax.broadcasted_iota(jnp.int32, (PAIR, PAIR), 0)
    ci = lax.broadcasted_iota(jnp.int32, (PAIR, PAIR), 1)
    same_chunk = (ri >= CHUNK) == (ci >= CHUNK)
    low_half = lax.broadcasted_iota(jnp.int32, (npb, PAIR), 1) < CHUNK
    ones_b = jnp.ones((PAIR, LANES), BF16)

    def valid_mask(d, par):
        rel = li - CHUNK * par
        return (rel >= ti) & (rel < CHUNK) if d else (rel <= ti) & (rel >= 0)

    for d, g_ref in ((0, gf_ref), (1, gb_ref)):
        last = 0 if d else CHUNK - 1
        xf = g_ref[2 * d + 1] + bias_ref[head, 2 * d + 1]
        lf = (jnp.minimum(xf, 0.0) - jnp.log(1.0 + jnp.exp(-jnp.abs(xf)))) * LOG2E
        cmat = jnp.where(same_chunk & ((ri >= ci) if d else (ri <= ci)), 1.0, 0.0).astype(BF16)
        b = sum(_dot(t, cmat) for t in _split3(lf))
        a = (g_ref[2 * d] + bias_ref[head, 2 * d]) * LOG2E - b
        a_scr[d] = a
        lf_scr[d] = lf
        amax = [jnp.broadcast_to(jnp.max(jnp.where(low_half == (par == 0), a, -jnp.inf), axis=1, keepdims=True),
                                 (npb, LANES)) for par in (0, 1)]
        blast = [jnp.broadcast_to(b[:, CHUNK * par + last:CHUNK * par + last + 1], (npb, LANES)) for par in (0, 1)]
        m = m_scr[d:d + 1, :]
        for p in (range(npb - 1, -1, -1) if d else range(npb)):
            for par in ((1, 0) if d else (0, 1)):
                c = 2 * p + par
                mc_scr[d, c:c + 1, :] = m
                m = blast[par][p:p + 1, :] + jnp.maximum(m, amax[par][p:p + 1, :])
        m_scr[d:d + 1, :] = m

    dirs = ((qf_ref, vf_ref, ktf_ref, hf_ref), (qb_ref, vb_ref, ktb_ref, hb_ref))

    def rows_of(c):
        return pl.ds(pl.multiple_of(c * CHUNK, CHUNK), CHUNK)

    def stage1(d, p, par):
        q_ref, _, kt_ref, _ = dirs[d]
        c = 2 * p + par
        valid = valid_mask(d, par)
        last = 0 if d else CHUNK - 1
        a_row = a_scr[d, pl.ds(p, 1), :]
        lf_row = lf_scr[d, pl.ds(p, 1), :]
        m = mc_scr[d, pl.ds(c, 1), 0:1]
        cm = jnp.max(jnp.where(valid, a_row, -jnp.inf), axis=1, keepdims=True)
        bcol = jnp.sum(jnp.where(valid, lf_row, 0.0), axis=1, keepdims=True)
        mc = jnp.maximum(cm, m)
        dmat = jnp.where(valid, jnp.exp2(a_row - mc), 0.0)
        inter = jnp.broadcast_to(jnp.exp2(m - mc), (CHUNK, LANES))
        emt = jnp.broadcast_to(jnp.exp2(-(bcol + mc)), (CHUNK, LANES))
        dec = jnp.broadcast_to(jnp.exp2(m - mc[last:last + 1]), (1, LANES))
        sd = (_dot(q_ref[rows_of(c), :], kt_ref[p]) * dmat).astype(BF16)
        return sd, dmat[last:last + 1, :], inter, emt, dec

    def stage2(d, p, par, pre):
        sd, ws_row, inter, emt, dec = pre
        q_ref, v_ref, kt_ref, h_ref = dirs[d]
        rows = rows_of(2 * p + par)
        q = q_ref[rows, :]
        v_pair = v_ref[pl.ds(pl.multiple_of(p * PAIR, PAIR), PAIR), :]
        v_aug = jnp.concatenate([v_pair, ones_b], axis=1)
        cta = cta_scr[d]
        ktw = (kt_ref[p].astype(F32) * ws_row).astype(BF16)
        both = _dot(jnp.concatenate([sd, ktw], axis=0), v_aug)
        na = both[:CHUNK] + jnp.concatenate([inter, inter, inter], axis=1) * _dot(q, cta.astype(BF16))
        r = 1.0 / jnp.maximum(jnp.abs(na[:, DH_M:]), emt)
        h_ref[rows, :] = (na[:, :DH_M] * jnp.concatenate([r, r], axis=1)).astype(BF16)
        cta_scr[d] = jnp.concatenate([dec, dec, dec], axis=1) * cta + both[CHUNK:]

    def body(i, carry):
        pf = i
        pb = npb - 1 - i
        mid_f = stage1(0, pf, 1)
        mid_b = stage1(1, pb, 0)
        stage2(0, pf, 0, carry[0])
        stage2(1, pb, 1, carry[1])
        nxt_f = stage1(0, jnp.minimum(pf + 1, npb - 1), 0)
        nxt_b = stage1(1, jnp.maximum(pb - 1, 0), 1)
        stage2(0, pf, 1, mid_f)
        stage2(1, pb, 0, mid_b)
        return nxt_f, nxt_b

    lax.fori_loop(0, npb, body, (stage1(0, 0, 0), stage1(1, npb - 1, 1)), unroll=MLSTM_UNROLL)


def _mlstm(bias, q, v, kt, gates, sb):
    bsz, seq, _ = q.shape
    nb = seq // sb
    npb = sb // PAIR
    fwd = lambda b, h, j: (b, j, h)
    bwd = lambda b, h, j: (b, nb - 1 - j, h)
    kfwd = lambda b, h, j: (h, b * nb + j, 0, 0)
    kbwd = lambda b, h, j: (h, b * nb + nb - 1 - j, 0, 0)
    gfwd = lambda b, h, j: (b, h, 0, j, 0)
    gbwd = lambda b, h, j: (b, h, 0, nb - 1 - j, 0)
    tok = lambda im: pl.BlockSpec((None, sb, DH_M), im)
    ktb = lambda im: pl.BlockSpec((None, npb, DH_M, PAIR), im)
    gb = lambda im: pl.BlockSpec((None, None, 4, npb, PAIR), im)
    out = jax.ShapeDtypeStruct((bsz, seq, W_M), BF16)
    return pl.pallas_call(
        functools.partial(_mlstm_kernel, npb=npb),
        grid=(bsz, N_HEADS_M, nb),
        in_specs=[pl.BlockSpec(memory_space=pltpu.SMEM),
                  tok(fwd), tok(fwd), ktb(kfwd), gb(gfwd),
                  tok(bwd), tok(bwd), ktb(kbwd), gb(gbwd)],
        out_specs=[tok(fwd), tok(bwd)],
        out_shape=[out, out],
        scratch_shapes=[pltpu.VMEM((2, DH_M, DH_M + LANES), F32), pltpu.VMEM((8, LANES), F32),
                        pltpu.VMEM((2, npb, PAIR), F32), pltpu.VMEM((2, npb, PAIR), F32),
                        pltpu.VMEM((2, 2 * npb, LANES), F32)],
        compiler_params=_cparams(("parallel", "parallel", "arbitrary")),
        name="mlstm",
    )(bias, q, v, kt, gates, q, v, kt, gates)


def _attn_kernel(qt_ref, k_ref, vt_ref, eye_ref, o_ref, st0, st1, *, nk):
    st_scr = (st0, st1)
    tq = qt_ref.shape[-1]
    tk = k_ref.shape[1]
    qs = [qt_ref[g] for g in range(GROUP_A)]

    def tick(t, slot, do_qk, do_sm, state):
        pmax, ms, accs = state
        n_pmax, n_ms, n_accs = list(pmax), list(ms), list(accs)
        for g in range(GROUP_A):
            if do_sm:
                m_new = jnp.maximum(ms[g], jnp.max(pmax[g], axis=0, keepdims=True))
                acc = jnp.exp2(ms[g] - m_new) * accs[g]
                n_ms[g] = m_new
            pm = None
            pts = []
            for c in range(tk // ATT_CHUNK):
                rows = slice(c * ATT_CHUNK, (c + 1) * ATT_CHUNK)
                if do_qk:
                    st = _dot(k_ref[t, rows, :], qs[g])
                    st_scr[slot][g, rows, :] = st
                    cm = jnp.max(st.reshape(ATT_CHUNK // 8, 8, tq), axis=0)
                    pm = cm if pm is None else jnp.maximum(pm, cm)
                if do_sm:
                    pts.append(jnp.exp2(st_scr[1 - slot][g, rows, :] - m_new).astype(BF16))
                    if (c + 1) * ATT_CHUNK % MXU_DIM == 0:
                        kt = (c + 1) * ATT_CHUNK // MXU_DIM - 1
                        cols = slice(kt * MXU_DIM, (kt + 1) * MXU_DIM)
                        acc = acc + _dot(vt_ref[t - 1, :, cols], jnp.concatenate(pts, axis=0))
                        pts = []
            if do_qk:
                n_pmax[g] = pm
            if do_sm:
                n_accs[g] = acc
        return tuple(n_pmax), tuple(n_ms), tuple(n_accs)

    neg = tuple(jnp.full((1, tq), -jnp.inf, F32) for _ in range(GROUP_A))
    neg8 = tuple(jnp.full((8, tq), -jnp.inf, F32) for _ in range(GROUP_A))
    state = (neg8, neg, tuple(jnp.zeros((VT_ROWS, tq), F32) for _ in range(GROUP_A)))
    state = tick(0, 0, True, False, state)
    state = tick(1, 1, True, True, state)
    ntrip = (nk - 2) // ATT_TICKS

    def body(i, state):
        for u in range(ATT_TICKS):
            state = tick(ATT_TICKS * i + 2 + u, u % 2, True, True, state)
        return state

    state = lax.fori_loop(0, ntrip, body, state)
    for t in range(2 + ntrip * ATT_TICKS, nk):
        state = tick(t, t % 2, True, True, state)
    state = tick(nk, nk % 2, False, True, state)
    ot = jnp.concatenate(
        [(acc[:DH_A] * (1.0 / acc[DH_A:DH_A + 1])).astype(BF16) for acc in state[2]], axis=0)
    o_ref[...] = _dot_nt(eye_ref[...], ot).astype(BF16)


def _attn(qt, kc, vtc, eye, tq):
    bsz, _, _, seq = qt.shape
    nk, tk = kc.shape[2], kc.shape[3]
    assert nk >= 2 and tk % MXU_DIM == 0
    return pl.pallas_call(
        functools.partial(_attn_kernel, nk=nk),
        grid=(bsz, N_KV_A, seq // tq),
        in_specs=[pl.BlockSpec((None, GROUP_A, DH_A, tq), lambda b, h, i: (b, h, 0, i)),
                  pl.BlockSpec((None, None, nk, tk, DH_A), lambda b, h, i: (b, h, 0, 0, 0)),
                  pl.BlockSpec((None, None, nk, VT_ROWS, tk), lambda b, h, i: (b, h, 0, 0, 0)),
                  pl.BlockSpec((tq, tq), lambda b, h, i: (0, 0))],
        out_specs=pl.BlockSpec((None, tq, GROUP_A * DH_A), lambda b, h, i: (b, i, h)),
        out_shape=jax.ShapeDtypeStruct((bsz, seq, W_A), BF16),
        scratch_shapes=[pltpu.VMEM((GROUP_A, tk, tq), F32), pltpu.VMEM((GROUP_A, tk, tq), F32)],
        compiler_params=_cparams(("parallel", "parallel", "arbitrary")),
        name="attn",
    )(qt, kc, vtc, eye)


def _final_kernel(x_ref, hf_ref, hb_ref, gm_ref, att_ref, gpre_ref, gfin_ref,
                  wz_ref, wmo_ref, wao_ref, wout_ref, y_ref):
    x = x_ref[...]
    h = _rms_bf16(x, gpre_ref[...])
    z_a = _dot(h, wz_ref[:, :W_A])
    um = ((hf_ref[...].astype(F32) + hb_ref[...].astype(F32)) * gm_ref[...].astype(F32)).astype(BF16)
    ua = (att_ref[...].astype(F32) * (z_a * jax.nn.sigmoid(z_a))).astype(BF16)
    y_m = _dot(um, wmo_ref[...])
    y_a = _dot(ua, wao_ref[...])
    g_m = _dot(h, wz_ref[:, W_A:W_A + D_MODEL])
    g_a = _dot(h, wz_ref[:, W_A + D_MODEL:])
    merged = jax.nn.sigmoid(g_m) * y_m + jax.nn.sigmoid(g_a) * y_a
    out = x + _dot(merged.astype(BF16), wout_ref[...])
    ms = jnp.mean(out * out, axis=-1, keepdims=True)
    y_ref[...] = out * lax.rsqrt(ms + EPS) * gfin_ref[...]


def _final(x2, hf, hb, gm, att, gpre, gfin, wz, wmo, wao, wout, tm):
    t = x2.shape[0]
    tok = lambda i: (i, 0)
    fixed = lambda i: (0, 0)
    act = pl.BlockSpec((tm, D_MODEL), tok)
    sq = pl.BlockSpec((D_MODEL, D_MODEL), fixed)
    vec = pl.BlockSpec((1, D_MODEL), fixed)
    return pl.pallas_call(
        _final_kernel,
        grid=(t // tm,),
        in_specs=[act, act, act, act, act, vec, vec, pl.BlockSpec((D_MODEL, 3 * D_MODEL), fixed), sq, sq, sq],
        out_specs=act,
        out_shape=jax.ShapeDtypeStruct((t, D_MODEL), F32),
        compiler_params=_cparams(("parallel",)),
        name="final",
    )(x2, hf, hb, gm, att, gpre, gfin, wz, wmo, wao, wout)


def _rope_tables(seq, q_gain, k_gain):
    half = AXIS_DIM // 2
    d = np.arange(DH_A)
    axis = d // AXIS_DIM
    lo = (d % AXIS_DIM) < half
    idx = (d % AXIS_DIM) % half
    partner = np.where(lo, d + half, d - half)
    freqs = ROPE_THETA ** (-jnp.arange(0, AXIS_DIM, 2, dtype=F32) / AXIS_DIM)
    s = jnp.arange(seq)
    pos = jnp.where(jnp.asarray(axis)[:, None] == 0, (s // GRID_W)[None, :], (s % GRID_W)[None, :])
    ang = pos.astype(F32) * freqs[jnp.asarray(idx)][:, None]
    cos, sin = jnp.cos(ang), jnp.sin(ang)
    sign = jnp.where(jnp.asarray(lo), -1.0, 1.0)[:, None]

    def tables(gain, scale):
        g = gain.astype(F32) * scale
        return g[:, None] * cos, sign * g[jnp.asarray(partner)][:, None] * sin

    qa, qb = tables(q_gain, LOG2E * DH_A ** -0.5)
    ka, kb = tables(k_gain, 1.0)
    pad = lambda t: jnp.pad(t.T, ((0, 0), (0, LANES - DH_A)))
    lo_row = jnp.asarray(lo)[:, None]
    return qa, qb, pad(ka), pad(jnp.where(lo_row, kb, 0.0)), pad(jnp.where(lo_row, 0.0, kb))


def _prep_weights(w_in, b_if, norm_pre, w_mo, w_ao, w_out, norm_final):
    c0 = 5 * W_M
    c1 = c0 + 4 * N_HEADS_M
    c2 = c1 + W_A
    c3 = c2 + W_KV
    c4 = c3 + W_KV
    wb = w_in.astype(BF16)
    blk = np.arange(256) // LANES
    wk = wb[:, c2:c3].reshape(D_MODEL, N_KV_A, DH_A)
    return dict(
        w_m=jnp.concatenate([wb[:, :W_M], wb[:, 2 * W_M:c0]], axis=1),
        w_kt=wb[:, W_M:2 * W_M].T,
        w_if=jnp.pad(wb[:, c0:c1], ((0, 0), (0, LANES - 4 * N_HEADS_M))),
        w_qt=wb[:, c1:c2].T,
        w_kp=jnp.pad(wk, ((0, 0), (0, 0), (0, LANES - DH_A))).reshape(D_MODEL, N_KV_A * LANES),
        w_vt=wb[:, c3:c4].T,
        w_z=wb[:, c4:],
        bias=jnp.transpose(b_if.astype(F32), (2, 0, 1)).reshape(N_HEADS_M, 4),
        g_pre=norm_pre.astype(F32).reshape(1, D_MODEL),
        g_fin=norm_final.astype(F32).reshape(1, D_MODEL),
        ones=jnp.asarray(blk[:, None] == blk[None, :], dtype=BF16),
        w_mo=w_mo.astype(BF16), w_ao=w_ao.astype(BF16), w_out=w_out.astype(BF16),
    )


def _tiles(seq):
    tq = 512 if seq // 512 >= 4 * ATT_TICKS else 256
    return min(512, seq), min(2048, seq), min(tq, seq), min(512, seq)


def _layer(x, p, q_gain, k_gain):
    bsz, seq, _ = x.shape
    tm, sb, tq, tk = _tiles(seq)
    x2 = x.reshape(bsz * seq, D_MODEL)

    q_m, kt, v_m, gate_m, if_raw = _mproj(x2, p["g_pre"], p["w_m"], p["w_kt"], p["w_if"], tm)
    qt, kc, vtc = _aproj(x, p["g_pre"], p["w_qt"], p["w_kp"], p["w_vt"], p["ones"],
                         *_rope_tables(seq, q_gain, k_gain), tk)

    shp = (bsz, seq, W_M)
    gates = if_raw[:, :4 * N_HEADS_M].reshape(bsz, seq // PAIR, PAIR, 4, N_HEADS_M)
    gates = jnp.transpose(gates, (0, 4, 3, 1, 2))
    h_f, h_b = _mlstm(p["bias"], q_m.reshape(shp), v_m.reshape(shp), kt, gates, sb)

    att = _attn(qt, kc, vtc, jnp.eye(tq, dtype=BF16), tq)

    y = _final(x2, h_f.reshape(-1, W_M), h_b.reshape(-1, W_M), gate_m, att.reshape(-1, W_A),
               p["g_pre"], p["g_fin"], p["w_z"], p["w_mo"], p["w_ao"], p["w_out"], min(256, seq))
    return y.reshape(bsz, seq, D_MODEL)


def kernel(x_prompt, x_sample, w_in, b_if, norm_pre, q_gain, k_gain, w_mo, w_ao, w_out, norm_final):
    assert w_in.shape[0] == 1, "single-layer model"
    p = _prep_weights(w_in[0], b_if[0], norm_pre[0], w_mo[0], w_ao[0], w_out[0], norm_final)
    return (_layer(x_prompt, p, q_gain[0], k_gain[0]), _layer(x_sample, p, q_gain[0], k_gain[0]))
```

```python
import functools

import numpy as np
import jax
import jax.numpy as jnp
from jax import lax
from jax.experimental import pallas as pl
from jax.experimental.pallas import tpu as pltpu

F32 = jnp.float32
BF16 = jnp.bfloat16

D_MODEL = 1024
N_HEADS_M = 4
DH_M = 256
W_M = N_HEADS_M * DH_M
CHUNK = 64
N_HEADS_A = 16
N_KV_A = 4
GROUP_A = N_HEADS_A // N_KV_A
DH_A = 64
W_A = N_HEADS_A * DH_A
W_KV = N_KV_A * DH_A
GRID_W = 64
AXIS_DIM = DH_A // 2
ROPE_THETA = 10000.0
EPS = 1e-6

LANES = 128
VT_ROWS = 80
MXU_DIM = 256
ATT_CHUNK = 128
ATT_TICKS = 4
LOG2E = 1.4426950408889634
PAIR = 2 * CHUNK
MLSTM_UNROLL = 2
VMEM_LIMIT = 56 * 1024 * 1024


def _cparams(sem):
    return pltpu.CompilerParams(dimension_semantics=sem, vmem_limit_bytes=VMEM_LIMIT)


def _rms_bf16(x, g):
    ms = jnp.mean(x * x, axis=-1, keepdims=True)
    return (x * lax.rsqrt(ms + EPS) * g).astype(BF16)


def _dot(a, b):
    return jnp.dot(a, b, preferred_element_type=F32)


def _dot_nt(a, b):
    return lax.dot_general(a, b, (((1,), (1,)), ((), ())), preferred_element_type=F32)


def _mproj_kernel(x_ref, g_ref, w_ref, wkt_ref, wif_ref, q_ref, kt_ref, v_ref, gate_ref, if_ref):
    tm = x_ref.shape[0]
    h = _rms_bf16(x_ref[...], g_ref[...])

    def proj(c):
        return _dot(h, w_ref[:, c * W_M:(c + 1) * W_M])

    q_ref[...] = proj(0).astype(BF16)
    v_ref[...] = proj(1).astype(BF16)
    o = proj(2)
    z = proj(3)
    gate_ref[...] = (jax.nn.sigmoid(o) * (z * jax.nn.sigmoid(z))).astype(BF16)
    if_ref[...] = _dot(h, wif_ref[...])
    kt = _dot_nt(wkt_ref[...], h) * (DH_M ** -0.5)
    for hh in range(N_HEADS_M):
        for p in range(tm // PAIR):
            kt_ref[hh, p] = kt[hh * DH_M:(hh + 1) * DH_M, p * PAIR:(p + 1) * PAIR].astype(BF16)


def _mproj(x2, g, w, wkt, wif, tm):
    t = x2.shape[0]
    tok = lambda i: (i, 0)
    fixed = lambda i: (0, 0)
    big = jax.ShapeDtypeStruct((t, W_M), BF16)
    return pl.pallas_call(
        _mproj_kernel,
        grid=(t // tm,),
        in_specs=[pl.BlockSpec((tm, D_MODEL), tok), pl.BlockSpec((1, D_MODEL), fixed),
                  pl.BlockSpec((D_MODEL, 4 * W_M), fixed), pl.BlockSpec((W_M, D_MODEL), fixed),
                  pl.BlockSpec((D_MODEL, LANES), fixed)],
        out_specs=[pl.BlockSpec((tm, W_M), tok),
                   pl.BlockSpec((N_HEADS_M, tm // PAIR, DH_M, PAIR), lambda i: (0, i, 0, 0)),
                   pl.BlockSpec((tm, W_M), tok), pl.BlockSpec((tm, W_M), tok), pl.BlockSpec((tm, LANES), tok)],
        out_shape=[big, jax.ShapeDtypeStruct((N_HEADS_M, t // PAIR, DH_M, PAIR), BF16), big, big,
                   jax.ShapeDtypeStruct((t, LANES), F32)],
        compiler_params=_cparams(("parallel",)),
        name="mproj",
    )(x2, g, w, wkt, wif)


def _aproj_kernel(x_ref, g_ref, wqt_ref, wkp_ref, wvt_ref, ones_ref, qa_ref, qb_ref, ka_ref, kup_ref, kdn_ref,
                  qt_ref, k_ref, vt_ref):
    tm = x_ref.shape[0]
    h = _rms_bf16(x_ref[...], g_ref[...])
    half = AXIS_DIM // 2

    qt = _dot_nt(wqt_ref[...], h).reshape(N_HEADS_A, DH_A, tm)
    xh = qt * lax.rsqrt(jnp.mean(qt * qt, axis=1, keepdims=True) + EPS)
    x5 = xh.reshape(N_HEADS_A, 2, 2, half, tm)
    partner = jnp.concatenate([x5[:, :, 1:2], x5[:, :, 0:1]], axis=2).reshape(N_HEADS_A, DH_A, tm)
    qt_ref[...] = (xh * qa_ref[...] + partner * qb_ref[...]).astype(BF16)

    kp = _dot(h, wkp_ref[...])
    ones = ones_ref[...]
    ka, kup, kdn = ka_ref[...], kup_ref[...], kdn_ref[...]
    for c in range(N_KV_A // 2):
        blk = kp[:, c * MXU_DIM:(c + 1) * MXU_DIM]
        ss = _dot((blk * blk).astype(BF16), ones)
        y = blk * lax.rsqrt(ss * (1.0 / DH_A) + EPS)
        for j in range(2):
            yc = y[:, j * LANES:(j + 1) * LANES]
            up = pltpu.roll(yc, LANES - half, 1)
            dn = pltpu.roll(yc, half, 1)
            k_ref[2 * c + j] = (yc * ka + up * kup + dn * kdn)[:, :DH_A].astype(BF16)

    vt = _dot_nt(wvt_ref[...], h)
    row = lax.broadcasted_iota(jnp.int32, (VT_ROWS - DH_A, tm), 0)
    extra = jnp.where(row == 0, 1.0, 0.0).astype(BF16)
    for hh in range(N_KV_A):
        vt_ref[hh, :DH_A, :] = vt[hh * DH_A:(hh + 1) * DH_A, :].astype(BF16)
        vt_ref[hh, DH_A:, :] = extra


def _aproj(x3, g, wqt, wkp, wvt, ones, qa, qb, ka, kup, kdn, tk):
    bsz, seq, _ = x3.shape
    nk = seq // tk
    fixed = lambda b, j: (0, 0)
    return pl.pallas_call(
        _aproj_kernel,
        grid=(bsz, nk),
        in_specs=[pl.BlockSpec((None, tk, D_MODEL), lambda b, j: (b, j, 0)), pl.BlockSpec((1, D_MODEL), fixed),
                  pl.BlockSpec((W_A, D_MODEL), fixed), pl.BlockSpec((D_MODEL, N_KV_A * LANES), fixed),
                  pl.BlockSpec((W_KV, D_MODEL), fixed), pl.BlockSpec((MXU_DIM, MXU_DIM), fixed),
                  pl.BlockSpec((DH_A, tk), lambda b, j: (0, j)), pl.BlockSpec((DH_A, tk), lambda b, j: (0, j)),
                  pl.BlockSpec((tk, LANES), lambda b, j: (j, 0)), pl.BlockSpec((tk, LANES), lambda b, j: (j, 0)),
                  pl.BlockSpec((tk, LANES), lambda b, j: (j, 0))],
        out_specs=[pl.BlockSpec((None, N_HEADS_A, DH_A, tk), lambda b, j: (b, 0, 0, j)),
                   pl.BlockSpec((None, N_KV_A, None, tk, DH_A), lambda b, j: (b, 0, j, 0, 0)),
                   pl.BlockSpec((None, N_KV_A, None, VT_ROWS, tk), lambda b, j: (b, 0, j, 0, 0))],
        out_shape=[jax.ShapeDtypeStruct((bsz, N_HEADS_A, DH_A, seq), BF16),
                   jax.ShapeDtypeStruct((bsz, N_KV_A, nk, tk, DH_A), BF16),
                   jax.ShapeDtypeStruct((bsz, N_KV_A, nk, VT_ROWS, tk), BF16)],
        compiler_params=_cparams(("parallel", "parallel")),
        name="aproj",
    )(x3, g, wqt, wkp, wvt, ones, qa, qb, ka, kup, kdn)


def _split3(x):
    h1 = x.astype(BF16)
    r1 = x - h1.astype(F32)
    h2 = r1.astype(BF16)
    return h1, h2, (r1 - h2.astype(F32)).astype(BF16)


def _mlstm_kernel(bias_ref,
                  qf_ref, vf_ref, ktf_ref, gf_ref,
                  qb_ref, vb_ref, ktb_ref, gb_ref,
                  hf_ref, hb_ref,
                  cta_scr, m_scr, a_scr, lf_scr, mc_scr, *, npb):
    head = pl.program_id(1)

    @pl.when(pl.program_id(2) == 0)
    def _():
        cta_scr[...] = jnp.zeros_like(cta_scr)
        m_scr[...] = jnp.zeros_like(m_scr)

    ti = lax.broadcasted_iota(jnp.int32, (CHUNK, PAIR), 0)
    li = lax.broadcasted_iota(jnp.int32, (CHUNK, PAIR), 1)
    ri = lax.broadcasted_iota(jnp.int32, (PAIR, PAIR), 0)
    ci = lax.broadcasted_iota(jnp.int32, (PAIR, PAIR), 1)
    same_chunk = (ri >= CHUNK) == (ci >= CHUNK)
    low_half = lax.broadcasted_iota(jnp.int32, (npb, PAIR), 1) < CHUNK
    ones_b = jnp.ones((PAIR, LANES), BF16)

    def valid_mask(d, par):
        rel = li - CHUNK * par
        return (rel >= ti) & (rel < CHUNK) if d else (rel <= ti) & (rel >= 0)

    for d, g_ref in ((0, gf_ref), (1, gb_ref)):
        last = 0 if d else CHUNK - 1
        xf = g_ref[2 * d + 1] + bias_ref[head, 2 * d + 1]
        lf = (jnp.minimum(xf, 0.0) - jnp.log(1.0 + jnp.exp(-jnp.abs(xf)))) * LOG2E
        cmat = jnp.where(same_chunk & ((ri >= ci) if d else (ri <= ci)), 1.0, 0.0).astype(BF16)
        b = sum(_dot(t, cmat) for t in _split3(lf))
        a = (g_ref[2 * d] + bias_ref[head, 2 * d]) * LOG2E - b
        a_scr[d] = a
        lf_scr[d] = lf
        amax = [jnp.broadcast_to(jnp.max(jnp.where(low_half == (par == 0), a, -jnp.inf), axis=1, keepdims=True),
                                 (npb, LANES)) for par in (0, 1)]
        blast = [jnp.broadcast_to(b[:, CHUNK * par + last:CHUNK * par + last + 1], (npb, LANES)) for par in (0, 1)]
        m = m_scr[d:d + 1, :]
        for p in (range(npb - 1, -1, -1) if d else range(npb)):
            for par in ((1, 0) if d else (0, 1)):
                c = 2 * p + par
                mc_scr[d, c:c + 1, :] = m
                m = blast[par][p:p + 1, :] + jnp.maximum(m, amax[par][p:p + 1, :])
        m_scr[d:d + 1, :] = m

    dirs = ((qf_ref, vf_ref, ktf_ref, hf_ref), (qb_ref, vb_ref, ktb_ref, hb_ref))

    def rows_of(c):
        return pl.ds(pl.multiple_of(c * CHUNK, CHUNK), CHUNK)

    def stage1(d, p, par):
        q_ref, _, kt_ref, _ = dirs[d]
        c = 2 * p + par
        valid = valid_mask(d, par)
        last = 0 if d else CHUNK - 1
        a_row = a_scr[d, pl.ds(p, 1), :]
        lf_row = lf_scr[d, pl.ds(p, 1), :]
        m = mc_scr[d, pl.ds(c, 1), 0:1]
        cm = jnp.max(jnp.where(valid, a_row, -jnp.inf), axis=1, keepdims=True)
        bcol = jnp.sum(jnp.where(valid, lf_row, 0.0), axis=1, keepdims=True)
        mc = jnp.maximum(cm, m)
        dmat = jnp.where(valid, jnp.exp2(a_row - mc), 0.0)
        inter = jnp.broadcast_to(jnp.exp2(m - mc), (CHUNK, LANES))
        emt = jnp.broadcast_to(jnp.exp2(-(bcol + mc)), (CHUNK, LANES))
        dec = jnp.broadcast_to(jnp.exp2(m - mc[last:last + 1]), (1, LANES))
        sd = (_dot(q_ref[rows_of(c), :], kt_ref[p]) * dmat).astype(BF16)
        return sd, dmat[last:last + 1, :], inter, emt, dec

    def stage2(d, p, par, pre):
        sd, ws_row, inter, emt, dec = pre
        q_ref, v_ref, kt_ref, h_ref = dirs[d]
        rows = rows_of(2 * p + par)
        q = q_ref[rows, :]
        v_pair = v_ref[pl.ds(pl.multiple_of(p * PAIR, PAIR), PAIR), :]
        v_aug = jnp.concatenate([v_pair, ones_b], axis=1)
        cta = cta_scr[d]
        ktw = (kt_ref[p].astype(F32) * ws_row).astype(BF16)
        both = _dot(jnp.concatenate([sd, ktw], axis=0), v_aug)
        na = both[:CHUNK] + jnp.concatenate([inter, inter, inter], axis=1) * _dot(q, cta.astype(BF16))
        r = 1.0 / jnp.maximum(jnp.abs(na[:, DH_M:]), emt)
        h_ref[rows, :] = (na[:, :DH_M] * jnp.concatenate([r, r], axis=1)).astype(BF16)
        cta_scr[d] = jnp.concatenate([dec, dec, dec], axis=1) * cta + both[CHUNK:]

    def body(i, carry):
        pf = i
        pb = npb - 1 - i
        mid_f = stage1(0, pf, 1)
        mid_b = stage1(1, pb, 0)
        stage2(0, pf, 0, carry[0])
        stage2(1, pb, 1, carry[1])
        nxt_f = stage1(0, jnp.minimum(pf + 1, npb - 1), 0)
        nxt_b = stage1(1, jnp.maximum(pb - 1, 0), 1)
        stage2(0, pf, 1, mid_f)
        stage2(1, pb, 0, mid_b)
        return nxt_f, nxt_b

    lax.fori_loop(0, npb, body, (stage1(0, 0, 0), stage1(1, npb - 1, 1)), unroll=MLSTM_UNROLL)


def _mlstm(bias, q, v, kt, gates, sb):
    bsz, seq, _ = q.shape
    nb = seq // sb
    npb = sb // PAIR
    fwd = lambda b, h, j: (b, j, h)
    bwd = lambda b, h, j: (b, nb - 1 - j, h)
    kfwd = lambda b, h, j: (h, b * nb + j, 0, 0)
    kbwd = lambda b, h, j: (h, b * nb + nb - 1 - j, 0, 0)
    gfwd = lambda b, h, j: (b, h, 0, j, 0)
    gbwd = lambda b, h, j: (b, h, 0, nb - 1 - j, 0)
    tok = lambda im: pl.BlockSpec((None, sb, DH_M), im)
    ktb = lambda im: pl.BlockSpec((None, npb, DH_M, PAIR), im)
    gb = lambda im: pl.BlockSpec((None, None, 4, npb, PAIR), im)
    out = jax.ShapeDtypeStruct((bsz, seq, W_M), BF16)
    return pl.pallas_call(
        functools.partial(_mlstm_kernel, npb=npb),
        grid=(bsz, N_HEADS_M, nb),
        in_specs=[pl.BlockSpec(memory_space=pltpu.SMEM),
                  tok(fwd), tok(fwd), ktb(kfwd), gb(gfwd),
                  tok(bwd), tok(bwd), ktb(kbwd), gb(gbwd)],
        out_specs=[tok(fwd), tok(bwd)],
        out_shape=[out, out],
        scratch_shapes=[pltpu.VMEM((2, DH_M, DH_M + LANES), F32), pltpu.VMEM((8, LANES), F32),
                        pltpu.VMEM((2, npb, PAIR), F32), pltpu.VMEM((2, npb, PAIR), F32),
                        pltpu.VMEM((2, 2 * npb, LANES), F32)],
        compiler_params=_cparams(("parallel", "parallel", "arbitrary")),
        name="mlstm",
    )(bias, q, v, kt, gates, q, v, kt, gates)


def _attn_kernel(qt_ref, k_ref, vt_ref, eye_ref, o_ref, st0, st1, *, nk):
    st_scr = (st0, st1)
    tq = qt_ref.shape[-1]
    tk = k_ref.shape[1]
    qs = [qt_ref[g] for g in range(GROUP_A)]

    def tick(t, slot, do_qk, do_sm, state):
        pmax, ms, accs = state
        n_pmax, n_ms, n_accs = list(pmax), list(ms), list(accs)
        for g in range(GROUP_A):
            if do_sm:
                m_new = jnp.maximum(ms[g], jnp.max(pmax[g], axis=0, keepdims=True))
                acc = jnp.exp2(ms[g] - m_new) * accs[g]
                n_ms[g] = m_new
            pm = None
            pts = []
            for c in range(tk // ATT_CHUNK):
                rows = slice(c * ATT_CHUNK, (c + 1) * ATT_CHUNK)
                if do_qk:
                    st = _dot(k_ref[t, rows, :], qs[g])
                    st_scr[slot][g, rows, :] = st
                    cm = jnp.max(st.reshape(ATT_CHUNK // 8, 8, tq), axis=0)
                    pm = cm if pm is None else jnp.maximum(pm, cm)
                if do_sm:
                    pts.append(jnp.exp2(st_scr[1 - slot][g, rows, :] - m_new).astype(BF16))
                    if (c + 1) * ATT_CHUNK % MXU_DIM == 0:
                        kt = (c + 1) * ATT_CHUNK // MXU_DIM - 1
                        cols = slice(kt * MXU_DIM, (kt + 1) * MXU_DIM)
                        acc = acc + _dot(vt_ref[t - 1, :, cols], jnp.concatenate(pts, axis=0))
                        pts = []
            if do_qk:
                n_pmax[g] = pm
            if do_sm:
                n_accs[g] = acc
        return tuple(n_pmax), tuple(n_ms), tuple(n_accs)

    neg = tuple(jnp.full((1, tq), -jnp.inf, F32) for _ in range(GROUP_A))
    neg8 = tuple(jnp.full((8, tq), -jnp.inf, F32) for _ in range(GROUP_A))
    state = (neg8, neg, tuple(jnp.zeros((VT_ROWS, tq), F32) for _ in range(GROUP_A)))
    state = tick(0, 0, True, False, state)
    state = tick(1, 1, True, True, state)
    ntrip = (nk - 2) // ATT_TICKS

    def body(i, state):
        for u in range(ATT_TICKS):
            state = tick(ATT_TICKS * i + 2 + u, u % 2, True, True, state)
        return state

    state = lax.fori_loop(0, ntrip, body, state)
    for t in range(2 + ntrip * ATT_TICKS, nk):
        state = tick(t, t % 2, True, True, state)
    state = tick(nk, nk % 2, False, True, state)
    ot = jnp.concatenate(
        [(acc[:DH_A] * (1.0 / acc[DH_A:DH_A + 1])).astype(BF16) for acc in state[2]], axis=0)
    o_ref[...] = _dot_nt(eye_ref[...], ot).astype(BF16)


def _attn(qt, kc, vtc, eye, tq):
    bsz, _, _, seq = qt.shape
    nk, tk = kc.shape[2], kc.shape[3]
    assert nk >= 2 and tk % MXU_DIM == 0
    return pl.pallas_call(
        functools.partial(_attn_kernel, nk=nk),
        grid=(bsz, N_KV_A, seq // tq),
        in_specs=[pl.BlockSpec((None, GROUP_A, DH_A, tq), lambda b, h, i: (b, h, 0, i)),
                  pl.BlockSpec((None, None, nk, tk, DH_A), lambda b, h, i: (b, h, 0, 0, 0)),
                  pl.BlockSpec((None, None, nk, VT_ROWS, tk), lambda b, h, i: (b, h, 0, 0, 0)),
                  pl.BlockSpec((tq, tq), lambda b, h, i: (0, 0))],
        out_specs=pl.BlockSpec((None, tq, GROUP_A * DH_A), lambda b, h, i: (b, i, h)),
        out_shape=jax.ShapeDtypeStruct((bsz, seq, W_A), BF16),
        scratch_shapes=[pltpu.VMEM((GROUP_A, tk, tq), F32), pltpu.VMEM((GROUP_A, tk, tq), F32)],
        compiler_params=_cparams(("parallel", "parallel", "arbitrary")),
        name="attn",
    )(qt, kc, vtc, eye)


def _final_kernel(x_ref, hf_ref, hb_ref, gm_ref, att_ref, gpre_ref, gfin_ref,
                  wz_ref, wmo_ref, wao_ref, wout_ref, y_ref):
    x = x_ref[...]
    h = _rms_bf16(x, gpre_ref[...])
    z_a = _dot(h, wz_ref[:, :W_A])
    um = ((hf_ref[...].astype(F32) + hb_ref[...].astype(F32)) * gm_ref[...].astype(F32)).astype(BF16)
    ua = (att_ref[...].astype(F32) * (z_a * jax.nn.sigmoid(z_a))).astype(BF16)
    y_m = _dot(um, wmo_ref[...])
    y_a = _dot(ua, wao_ref[...])
    g_m = _dot(h, wz_ref[:, W_A:W_A + D_MODEL])
    g_a = _dot(h, wz_ref[:, W_A + D_MODEL:])
    merged = jax.nn.sigmoid(g_m) * y_m + jax.nn.sigmoid(g_a) * y_a
    out = x + _dot(merged.astype(BF16), wout_ref[...])
    ms = jnp.mean(out * out, axis=-1, keepdims=True)
    y_ref[...] = out * lax.rsqrt(ms + EPS) * gfin_ref[...]


def _final(x2, hf, hb, gm, att, gpre, gfin, wz, wmo, wao, wout, tm):
    t = x2.shape[0]
    tok = lambda i: (i, 0)
    fixed = lambda i: (0, 0)
    act = pl.BlockSpec((tm, D_MODEL), tok)
    sq = pl.BlockSpec((D_MODEL, D_MODEL), fixed)
    vec = pl.BlockSpec((1, D_MODEL), fixed)
    return pl.pallas_call(
        _final_kernel,
        grid=(t // tm,),
        in_specs=[act, act, act, act, act, vec, vec, pl.BlockSpec((D_MODEL, 3 * D_MODEL), fixed), sq, sq, sq],
        out_specs=act,
        out_shape=jax.ShapeDtypeStruct((t, D_MODEL), F32),
        compiler_params=_cparams(("parallel",)),
        name="final",
    )(x2, hf, hb, gm, att, gpre, gfin, wz, wmo, wao, wout)


def _rope_tables(seq, q_gain, k_gain):
    half = AXIS_DIM // 2
    d = np.arange(DH_A)
    axis = d // AXIS_DIM
    lo = (d % AXIS_DIM) < half
    idx = (d % AXIS_DIM) % half
    partner = np.where(lo, d + half, d - half)
    freqs = ROPE_THETA ** (-jnp.arange(0, AXIS_DIM, 2, dtype=F32) / AXIS_DIM)
    s = jnp.arange(seq)
    pos = jnp.where(jnp.asarray(axis)[:, None] == 0, (s // GRID_W)[None, :], (s % GRID_W)[None, :])
    ang = pos.astype(F32) * freqs[jnp.asarray(idx)][:, None]
    cos, sin = jnp.cos(ang), jnp.sin(ang)
    sign = jnp.where(jnp.asarray(lo), -1.0, 1.0)[:, None]

    def tables(gain, scale):
        g = gain.astype(F32) * scale
        return g[:, None] * cos, sign * g[jnp.asarray(partner)][:, None] * sin

    qa, qb = tables(q_gain, LOG2E * DH_A ** -0.5)
    ka, kb = tables(k_gain, 1.0)
    pad = lambda t: jnp.pad(t.T, ((0, 0), (0, LANES - DH_A)))
    lo_row = jnp.asarray(lo)[:, None]
    return qa, qb, pad(ka), pad(jnp.where(lo_row, kb, 0.0)), pad(jnp.where(lo_row, 0.0, kb))


def _prep_weights(w_in, b_if, norm_pre, w_mo, w_ao, w_out, norm_final):
    c0 = 5 * W_M
    c1 = c0 + 4 * N_HEADS_M
    c2 = c1 + W_A
    c3 = c2 + W_KV
    c4 = c3 + W_KV
    wb = w_in.astype(BF16)
    blk = np.arange(MXU_DIM) // LANES
    wk = wb[:, c2:c3].reshape(D_MODEL, N_KV_A, DH_A)
    return dict(
        w_m=jnp.concatenate([wb[:, :W_M], wb[:, 2 * W_M:c0]], axis=1),
        w_kt=wb[:, W_M:2 * W_M].T,
        w_if=jnp.pad(wb[:, c0:c1], ((0, 0), (0, LANES - 4 * N_HEADS_M))),
        w_qt=wb[:, c1:c2].T,
        w_kp=jnp.pad(wk, ((0, 0), (0, 0), (0, LANES - DH_A))).reshape(D_MODEL, N_KV_A * LANES),
        w_vt=wb[:, c3:c4].T,
        w_z=wb[:, c4:],
        bias=jnp.transpose(b_if.astype(F32), (2, 0, 1)).reshape(N_HEADS_M, 4),
        g_pre=norm_pre.astype(F32).reshape(1, D_MODEL),
        g_fin=norm_final.astype(F32).reshape(1, D_MODEL),
        ones=jnp.asarray(blk[:, None] == blk[None, :], dtype=BF16),
        w_mo=w_mo.astype(BF16), w_ao=w_ao.astype(BF16), w_out=w_out.astype(BF16),
    )


def _tiles(seq):
    tk = min(2 * MXU_DIM, seq)
    tq = tk if seq // tk >= 4 * ATT_TICKS else tk // 2
    return min(512, seq), min(4096, seq), tq, tk, min(512, seq)


def _layer(x, p, q_gain, k_gain):
    bsz, seq, _ = x.shape
    tm, sb, tq, tk, tf = _tiles(seq)
    x2 = x.reshape(bsz * seq, D_MODEL)

    q_m, kt, v_m, gate_m, if_raw = _mproj(x2, p["g_pre"], p["w_m"], p["w_kt"], p["w_if"], tm)
    qt, kc, vtc = _aproj(x, p["g_pre"], p["w_qt"], p["w_kp"], p["w_vt"], p["ones"],
                         *_rope_tables(seq, q_gain, k_gain), tk)

    shp = (bsz, seq, W_M)
    gates = if_raw[:, :4 * N_HEADS_M].reshape(bsz, seq // PAIR, PAIR, 4, N_HEADS_M)
    gates = jnp.transpose(gates, (0, 4, 3, 1, 2))
    h_f, h_b = _mlstm(p["bias"], q_m.reshape(shp), v_m.reshape(shp), kt, gates, sb)

    att = _attn(qt, kc, vtc, jnp.eye(tq, dtype=BF16), tq)

    y = _final(x2, h_f.reshape(-1, W_M), h_b.reshape(-1, W_M), gate_m, att.reshape(-1, W_A),
               p["g_pre"], p["g_fin"], p["w_z"], p["w_mo"], p["w_ao"], p["w_out"], tf)
    return y.reshape(bsz, seq, D_MODEL)


def kernel(x_prompt, x_sample, w_in, b_if, norm_pre, q_gain, k_gain, w_mo, w_ao, w_out, norm_final):
    assert w_in.shape[0] == 1, "single-layer model"
    p = _prep_weights(w_in[0], b_if[0], norm_pre[0], w_mo[0], w_ao[0], w_out[0], norm_final)
    return (_layer(x_prompt, p, q_gain[0], k_gain[0]), _layer(x_sample, p, q_gain[0], k_gain[0]))
```

```python
import functools

import numpy as np
import jax
import jax.numpy as jnp
from jax import lax
from jax.experimental import pallas as pl
from jax.experimental.pallas import tpu as pltpu

F32 = jnp.float32
BF16 = jnp.bfloat16

D_MODEL = 1024
N_HEADS_M = 4
DH_M = 256
W_M = N_HEADS_M * DH_M
CHUNK = 64
N_HEADS_A = 16
N_KV_A = 4
GROUP_A = N_HEADS_A // N_KV_A
DH_A = 64
W_A = N_HEADS_A * DH_A
W_KV = N_KV_A * DH_A
GRID_W = 64
AXIS_DIM = DH_A // 2
ROPE_THETA = 10000.0
EPS = 1e-6

LANES = 128
VT_ROWS = 80
MXU_DIM = 256
ATT_CHUNK = 128
ATT_TICKS = 6
LOG2E = 1.4426950408889634
PAIR = 2 * CHUNK
MLSTM_UNROLL = 2
VMEM_LIMIT = 56 * 1024 * 1024


def _cparams(sem):
    return pltpu.CompilerParams(dimension_semantics=sem, vmem_limit_bytes=VMEM_LIMIT)


def _rms_bf16(x, g):
    ms = jnp.mean(x * x, axis=-1, keepdims=True)
    return (x * lax.rsqrt(ms + EPS) * g).astype(BF16)


def _dot(a, b):
    return jnp.dot(a, b, preferred_element_type=F32)


def _dot_nt(a, b):
    return lax.dot_general(a, b, (((1,), (1,)), ((), ())), preferred_element_type=F32)


def _mproj_kernel(x_ref, g_ref, w_ref, wkt_ref, wif_ref, q_ref, kt_ref, v_ref, gate_ref, if_ref):
    tm = x_ref.shape[0]
    h = _rms_bf16(x_ref[...], g_ref[...])

    def proj(c):
        return _dot(h, w_ref[:, c * W_M:(c + 1) * W_M])

    q_ref[...] = proj(0).astype(BF16)
    v_ref[...] = proj(1).astype(BF16)
    o = proj(2)
    z = proj(3)
    gate_ref[...] = (jax.nn.sigmoid(o) * (z * jax.nn.sigmoid(z))).astype(BF16)
    if_ref[...] = _dot(h, wif_ref[...])
    kt = _dot_nt(wkt_ref[...], h) * (DH_M ** -0.5)
    for hh in range(N_HEADS_M):
        for p in range(tm // PAIR):
            kt_ref[hh, p] = kt[hh * DH_M:(hh + 1) * DH_M, p * PAIR:(p + 1) * PAIR].astype(BF16)


def _mproj(x2, g, w, wkt, wif, tm):
    t = x2.shape[0]
    tok = lambda i: (i, 0)
    fixed = lambda i: (0, 0)
    big = jax.ShapeDtypeStruct((t, W_M), BF16)
    return pl.pallas_call(
        _mproj_kernel,
        grid=(t // tm,),
        in_specs=[pl.BlockSpec((tm, D_MODEL), tok), pl.BlockSpec((1, D_MODEL), fixed),
                  pl.BlockSpec((D_MODEL, 4 * W_M), fixed), pl.BlockSpec((W_M, D_MODEL), fixed),
                  pl.BlockSpec((D_MODEL, LANES), fixed)],
        out_specs=[pl.BlockSpec((tm, W_M), tok),
                   pl.BlockSpec((N_HEADS_M, tm // PAIR, DH_M, PAIR), lambda i: (0, i, 0, 0)),
                   pl.BlockSpec((tm, W_M), tok), pl.BlockSpec((tm, W_M), tok), pl.BlockSpec((tm, LANES), tok)],
        out_shape=[big, jax.ShapeDtypeStruct((N_HEADS_M, t // PAIR, DH_M, PAIR), BF16), big, big,
                   jax.ShapeDtypeStruct((t, LANES), F32)],
        compiler_params=_cparams(("parallel",)),
        name="mproj",
    )(x2, g, w, wkt, wif)


def _aproj_kernel(x_ref, g_ref, wqt_ref, wkp_ref, wvt_ref, ones_ref, qa_ref, qb_ref, ka_ref, kup_ref, kdn_ref,
                  qt_ref, k_ref, vt_ref):
    tm = x_ref.shape[0]
    h = _rms_bf16(x_ref[...], g_ref[...])
    half = AXIS_DIM // 2

    qt = _dot_nt(wqt_ref[...], h).reshape(N_HEADS_A, DH_A, tm)
    xh = qt * lax.rsqrt(jnp.mean(qt * qt, axis=1, keepdims=True) + EPS)
    x5 = xh.reshape(N_HEADS_A, 2, 2, half, tm)
    partner = jnp.concatenate([x5[:, :, 1:2], x5[:, :, 0:1]], axis=2).reshape(N_HEADS_A, DH_A, tm)
    qt_ref[...] = (xh * qa_ref[...] + partner * qb_ref[...]).astype(BF16)

    kp = _dot(h, wkp_ref[...])
    ones = ones_ref[...]
    ka, kup, kdn = ka_ref[...], kup_ref[...], kdn_ref[...]
    for c in range(N_KV_A // 2):
        blk = kp[:, c * MXU_DIM:(c + 1) * MXU_DIM]
        ss = _dot((blk * blk).astype(BF16), ones)
        y = blk * lax.rsqrt(ss * (1.0 / DH_A) + EPS)
        for j in range(2):
            yc = y[:, j * LANES:(j + 1) * LANES]
            up = pltpu.roll(yc, LANES - half, 1)
            dn = pltpu.roll(yc, half, 1)
            k_ref[2 * c + j] = (yc * ka + up * kup + dn * kdn)[:, :DH_A].astype(BF16)

    vt = _dot_nt(wvt_ref[...], h)
    row = lax.broadcasted_iota(jnp.int32, (VT_ROWS - DH_A, tm), 0)
    extra = jnp.where(row == 0, 1.0, 0.0).astype(BF16)
    for hh in range(N_KV_A):
        vt_ref[hh, :DH_A, :] = vt[hh * DH_A:(hh + 1) * DH_A, :].astype(BF16)
        vt_ref[hh, DH_A:, :] = extra


def _aproj(x3, g, wqt, wkp, wvt, ones, qa, qb, ka, kup, kdn, tk):
    bsz, seq, _ = x3.shape
    nk = seq // tk
    fixed = lambda b, j: (0, 0)
    return pl.pallas_call(
        _aproj_kernel,
        grid=(bsz, nk),
        in_specs=[pl.BlockSpec((None, tk, D_MODEL), lambda b, j: (b, j, 0)), pl.BlockSpec((1, D_MODEL), fixed),
                  pl.BlockSpec((W_A, D_MODEL), fixed), pl.BlockSpec((D_MODEL, N_KV_A * LANES), fixed),
                  pl.BlockSpec((W_KV, D_MODEL), fixed), pl.BlockSpec((MXU_DIM, MXU_DIM), fixed),
                  pl.BlockSpec((DH_A, tk), lambda b, j: (0, j)), pl.BlockSpec((DH_A, tk), lambda b, j: (0, j)),
                  pl.BlockSpec((tk, LANES), lambda b, j: (j, 0)), pl.BlockSpec((tk, LANES), lambda b, j: (j, 0)),
                  pl.BlockSpec((tk, LANES), lambda b, j: (j, 0))],
        out_specs=[pl.BlockSpec((None, N_HEADS_A, DH_A, tk), lambda b, j: (b, 0, 0, j)),
                   pl.BlockSpec((None, N_KV_A, None, tk, DH_A), lambda b, j: (b, 0, j, 0, 0)),
                   pl.BlockSpec((None, N_KV_A, None, VT_ROWS, tk), lambda b, j: (b, 0, j, 0, 0))],
        out_shape=[jax.ShapeDtypeStruct((bsz, N_HEADS_A, DH_A, seq), BF16),
                   jax.ShapeDtypeStruct((bsz, N_KV_A, nk, tk, DH_A), BF16),
                   jax.ShapeDtypeStruct((bsz, N_KV_A, nk, VT_ROWS, tk), BF16)],
        compiler_params=_cparams(("parallel", "parallel")),
        name="aproj",
    )(x3, g, wqt, wkp, wvt, ones, qa, qb, ka, kup, kdn)


def _split3(x):
    h1 = x.astype(BF16)
    r1 = x - h1.astype(F32)
    h2 = r1.astype(BF16)
    return h1, h2, (r1 - h2.astype(F32)).astype(BF16)


def _mlstm_kernel(bias_ref,
                  qf_ref, vf_ref, ktf_ref, gf_ref,
                  qb_ref, vb_ref, ktb_ref, gb_ref,
                  hf_ref, hb_ref,
                  cta_scr, m_scr, a_scr, lf_scr, mc_scr, *, npb):
    head = pl.program_id(1)

    @pl.when(pl.program_id(2) == 0)
    def _():
        cta_scr[...] = jnp.zeros_like(cta_scr)
        m_scr[...] = jnp.zeros_like(m_scr)

    ti = lax.broadcasted_iota(jnp.int32, (CHUNK, PAIR), 0)
    li = lax.broadcasted_iota(jnp.int32, (CHUNK, PAIR), 1)
    ri = lax.broadcasted_iota(jnp.int32, (PAIR, PAIR), 0)
    ci = lax.broadcasted_iota(jnp.int32, (PAIR, PAIR), 1)
    same_chunk = (ri >= CHUNK) == (ci >= CHUNK)
    low_half = lax.broadcasted_iota(jnp.int32, (npb, PAIR), 1) < CHUNK
    ones_b = jnp.ones((PAIR, LANES), BF16)

    def valid_mask(d, par):
        rel = li - CHUNK * par
        return (rel >= ti) & (rel < CHUNK) if d else (rel <= ti) & (rel >= 0)

    for d, g_ref in ((0, gf_ref), (1, gb_ref)):
        last = 0 if d else CHUNK - 1
        xf = g_ref[2 * d + 1] + bias_ref[head, 2 * d + 1]
        lf = (jnp.minimum(xf, 0.0) - jnp.log(1.0 + jnp.exp(-jnp.abs(xf)))) * LOG2E
        cmat = jnp.where(same_chunk & ((ri >= ci) if d else (ri <= ci)), 1.0, 0.0).astype(BF16)
        b = sum(_dot(t, cmat) for t in _split3(lf))
        a = (g_ref[2 * d] + bias_ref[head, 2 * d]) * LOG2E - b
        a_scr[d] = a
        lf_scr[d] = lf
        amax = [jnp.broadcast_to(jnp.max(jnp.where(low_half == (par == 0), a, -jnp.inf), axis=1, keepdims=True),
                                 (npb, LANES)) for par in (0, 1)]
        blast = [jnp.broadcast_to(b[:, CHUNK * par + last:CHUNK * par + last + 1], (npb, LANES)) for par in (0, 1)]
        m = m_scr[d:d + 1, :]
        for p in (range(npb - 1, -1, -1) if d else range(npb)):
            for par in ((1, 0) if d else (0, 1)):
                c = 2 * p + par
                mc_scr[d, c:c + 1, :] = m
                m = blast[par][p:p + 1, :] + jnp.maximum(m, amax[par][p:p + 1, :])
        m_scr[d:d + 1, :] = m

    dirs = ((qf_ref, vf_ref, ktf_ref, hf_ref), (qb_ref, vb_ref, ktb_ref, hb_ref))

    def rows_of(c):
        return pl.ds(pl.multiple_of(c * CHUNK, CHUNK), CHUNK)

    def stage1(d, p, par):
        q_ref, _, kt_ref, _ = dirs[d]
        c = 2 * p + par
        valid = valid_mask(d, par)
        last = 0 if d else CHUNK - 1
        a_row = a_scr[d, pl.ds(p, 1), :]
        lf_row = lf_scr[d, pl.ds(p, 1), :]
        m = mc_scr[d, pl.ds(c, 1), 0:1]
        cm = jnp.max(jnp.where(valid, a_row, -jnp.inf), axis=1, keepdims=True)
        bcol = jnp.sum(jnp.where(valid, lf_row, 0.0), axis=1, keepdims=True)
        mc = jnp.maximum(cm, m)
        dmat = jnp.where(valid, jnp.exp2(a_row - mc), 0.0)
        inter = jnp.broadcast_to(jnp.exp2(m - mc), (CHUNK, LANES))
        emt = jnp.broadcast_to(jnp.exp2(-(bcol + mc)), (CHUNK, LANES))
        dec = jnp.broadcast_to(jnp.exp2(m - mc[last:last + 1]), (1, LANES))
        sd = (_dot(q_ref[rows_of(c), :], kt_ref[p]) * dmat).astype(BF16)
        return sd, dmat[last:last + 1, :], inter, emt, dec

    def stage2(d, p, par, pre):
        sd, ws_row, inter, emt, dec = pre
        q_ref, v_ref, kt_ref, h_ref = dirs[d]
        rows = rows_of(2 * p + par)
        q = q_ref[rows, :]
        v_pair = v_ref[pl.ds(pl.multiple_of(p * PAIR, PAIR), PAIR), :]
        v_aug = jnp.concatenate([v_pair, ones_b], axis=1)
        cta = cta_scr[d]
        ktw = (kt_ref[p].astype(F32) * ws_row).astype(BF16)
        both = _dot(jnp.concatenate([sd, ktw], axis=0), v_aug)
        na = both[:CHUNK] + jnp.concatenate([inter, inter, inter], axis=1) * _dot(q, cta.astype(BF16))
        r = 1.0 / jnp.maximum(jnp.abs(na[:, DH_M:]), emt)
        h_ref[rows, :] = (na[:, :DH_M] * jnp.concatenate([r, r], axis=1)).astype(BF16)
        cta_scr[d] = jnp.concatenate([dec, dec, dec], axis=1) * cta + both[CHUNK:]

    def body(i, carry):
        pf = i
        pb = npb - 1 - i
        mid_f = stage1(0, pf, 1)
        mid_b = stage1(1, pb, 0)
        stage2(0, pf, 0, carry[0])
        stage2(1, pb, 1, carry[1])
        nxt_f = stage1(0, jnp.minimum(pf + 1, npb - 1), 0)
        nxt_b = stage1(1, jnp.maximum(pb - 1, 0), 1)
        stage2(0, pf, 1, mid_f)
        stage2(1, pb, 0, mid_b)
        return nxt_f, nxt_b

    lax.fori_loop(0, npb, body, (stage1(0, 0, 0), stage1(1, npb - 1, 1)), unroll=MLSTM_UNROLL)


def _mlstm(bias, q, v, kt, gates, sb):
    bsz, seq, _ = q.shape
    nb = seq // sb
    npb = sb // PAIR
    fwd = lambda b, h, j: (b, j, h)
    bwd = lambda b, h, j: (b, nb - 1 - j, h)
    kfwd = lambda b, h, j: (h, b * nb + j, 0, 0)
    kbwd = lambda b, h, j: (h, b * nb + nb - 1 - j, 0, 0)
    gfwd = lambda b, h, j: (b, h, 0, j, 0)
    gbwd = lambda b, h, j: (b, h, 0, nb - 1 - j, 0)
    tok = lambda im: pl.BlockSpec((None, sb, DH_M), im)
    ktb = lambda im: pl.BlockSpec((None, npb, DH_M, PAIR), im)
    gb = lambda im: pl.BlockSpec((None, None, 4, npb, PAIR), im)
    out = jax.ShapeDtypeStruct((bsz, seq, W_M), BF16)
    return pl.pallas_call(
        functools.partial(_mlstm_kernel, npb=npb),
        grid=(bsz, N_HEADS_M, nb),
        in_specs=[pl.BlockSpec(memory_space=pltpu.SMEM),
                  tok(fwd), tok(fwd), ktb(kfwd), gb(gfwd),
                  tok(bwd), tok(bwd), ktb(kbwd), gb(gbwd)],
        out_specs=[tok(fwd), tok(bwd)],
        out_shape=[out, out],
        scratch_shapes=[pltpu.VMEM((2, DH_M, DH_M + LANES), F32), pltpu.VMEM((8, LANES), F32),
                        pltpu.VMEM((2, npb, PAIR), F32), pltpu.VMEM((2, npb, PAIR), F32),
                        pltpu.VMEM((2, 2 * npb, LANES), F32)],
        compiler_params=_cparams(("parallel", "parallel", "arbitrary")),
        name="mlstm",
    )(bias, q, v, kt, gates, q, v, kt, gates)


def _attn_kernel(qt_ref, k_ref, vt_ref, eye_ref, o_ref, st0, st1, *, nk):
    st_scr = (st0, st1)
    tq = qt_ref.shape[-1]
    tk = k_ref.shape[1]
    qs = [qt_ref[g] for g in range(GROUP_A)]

    def tick(t, slot, do_qk, do_sm, state):
        pmax, ms, accs = state
        n_pmax, n_ms, n_accs = list(pmax), list(ms), list(accs)
        for g in range(GROUP_A):
            if do_sm:
                m_new = jnp.maximum(ms[g], jnp.max(pmax[g], axis=0, keepdims=True))
                acc = jnp.exp2(ms[g] - m_new) * accs[g]
                n_ms[g] = m_new
            pm = None
            pts = []
            for c in range(tk // ATT_CHUNK):
                rows = slice(c * ATT_CHUNK, (c + 1) * ATT_CHUNK)
                if do_qk:
                    st = _dot(k_ref[t, rows, :], qs[g])
                    st_scr[slot][g, rows, :] = st
                    cm = jnp.max(st.reshape(ATT_CHUNK // 8, 8, tq), axis=0)
                    pm = cm if pm is None else jnp.maximum(pm, cm)
                if do_sm:
                    pts.append(jnp.exp2(st_scr[1 - slot][g, rows, :] - m_new).astype(BF16))
                    if (c + 1) * ATT_CHUNK % MXU_DIM == 0:
                        kt = (c + 1) * ATT_CHUNK // MXU_DIM - 1
                        cols = slice(kt * MXU_DIM, (kt + 1) * MXU_DIM)
                        acc = acc + _dot(vt_ref[t - 1, :, cols], jnp.concatenate(pts, axis=0))
                        pts = []
            if do_qk:
                n_pmax[g] = pm
            if do_sm:
                n_accs[g] = acc
        return tuple(n_pmax), tuple(n_ms), tuple(n_accs)

    neg = tuple(jnp.full((1, tq), -jnp.inf, F32) for _ in range(GROUP_A))
    neg8 = tuple(jnp.full((8, tq), -jnp.inf, F32) for _ in range(GROUP_A))
    state = (neg8, neg, tuple(jnp.zeros((VT_ROWS, tq), F32) for _ in range(GROUP_A)))
    state = tick(0, 0, True, False, state)
    state = tick(1, 1, True, True, state)
    ntrip = (nk - 2) // ATT_TICKS

    def body(i, state):
        for u in range(ATT_TICKS):
            state = tick(ATT_TICKS * i + 2 + u, u % 2, True, True, state)
        return state

    state = lax.fori_loop(0, ntrip, body, state)
    for t in range(2 + ntrip * ATT_TICKS, nk):
        state = tick(t, t % 2, True, True, state)
    state = tick(nk, nk % 2, False, True, state)
    ot = jnp.concatenate(
        [(acc[:DH_A] * (1.0 / acc[DH_A:DH_A + 1])).astype(BF16) for acc in state[2]], axis=0)
    o_ref[...] = _dot_nt(eye_ref[...], ot).astype(BF16)


def _attn(qt, kc, vtc, eye, tq):
    bsz, _, _, seq = qt.shape
    nk, tk = kc.shape[2], kc.shape[3]
    assert nk >= 2 and tk % MXU_DIM == 0
    return pl.pallas_call(
        functools.partial(_attn_kernel, nk=nk),
        grid=(bsz, N_KV_A, seq // tq),
        in_specs=[pl.BlockSpec((None, GROUP_A, DH_A, tq), lambda b, h, i: (b, h, 0, i)),
                  pl.BlockSpec((None, None, nk, tk, DH_A), lambda b, h, i: (b, h, 0, 0, 0)),
                  pl.BlockSpec((None, None, nk, VT_ROWS, tk), lambda b, h, i: (b, h, 0, 0, 0)),
                  pl.BlockSpec((tq, tq), lambda b, h, i: (0, 0))],
        out_specs=pl.BlockSpec((None, tq, GROUP_A * DH_A), lambda b, h, i: (b, i, h)),
        out_shape=jax.ShapeDtypeStruct((bsz, seq, W_A), BF16),
        scratch_shapes=[pltpu.VMEM((GROUP_A, tk, tq), F32), pltpu.VMEM((GROUP_A, tk, tq), F32)],
        compiler_params=_cparams(("parallel", "parallel", "arbitrary")),
        name="attn",
    )(qt, kc, vtc, eye)


def _final_kernel(x_ref, hf_ref, hb_ref, gm_ref, att_ref, gpre_ref, gfin_ref,
                  wz_ref, wmo_ref, wao_ref, wout_ref, y_ref):
    x = x_ref[...]
    h = _rms_bf16(x, gpre_ref[...])
    z_a = _dot(h, wz_ref[:, :W_A])
    um = ((hf_ref[...].astype(F32) + hb_ref[...].astype(F32)) * gm_ref[...].astype(F32)).astype(BF16)
    ua = (att_ref[...].astype(F32) * (z_a * jax.nn.sigmoid(z_a))).astype(BF16)
    y_m = _dot(um, wmo_ref[...])
    y_a = _dot(ua, wao_ref[...])
    g_m = _dot(h, wz_ref[:, W_A:W_A + D_MODEL])
    g_a = _dot(h, wz_ref[:, W_A + D_MODEL:])
    merged = jax.nn.sigmoid(g_m) * y_m + jax.nn.sigmoid(g_a) * y_a
    out = x + _dot(merged.astype(BF16), wout_ref[...])
    ms = jnp.mean(out * out, axis=-1, keepdims=True)
    y_ref[...] = out * lax.rsqrt(ms + EPS) * gfin_ref[...]


def _final(x2, hf, hb, gm, att, gpre, gfin, wz, wmo, wao, wout, tm):
    t = x2.shape[0]
    tok = lambda i: (i, 0)
    fixed = lambda i: (0, 0)
    act = pl.BlockSpec((tm, D_MODEL), tok)
    sq = pl.BlockSpec((D_MODEL, D_MODEL), fixed)
    vec = pl.BlockSpec((1, D_MODEL), fixed)
    return pl.pallas_call(
        _final_kernel,
        grid=(t // tm,),
        in_specs=[act, act, act, act, act, vec, vec, pl.BlockSpec((D_MODEL, 3 * D_MODEL), fixed), sq, sq, sq],
        out_specs=act,
        out_shape=jax.ShapeDtypeStruct((t, D_MODEL), F32),
        compiler_params=_cparams(("parallel",)),
        name="final",
    )(x2, hf, hb, gm, att, gpre, gfin, wz, wmo, wao, wout)


def _rope_tables(seq, q_gain, k_gain):
    half = AXIS_DIM // 2
    d = np.arange(DH_A)
    axis = d // AXIS_DIM
    lo = (d % AXIS_DIM) < half
    idx = (d % AXIS_DIM) % half
    partner = np.where(lo, d + half, d - half)
    freqs = ROPE_THETA ** (-jnp.arange(0, AXIS_DIM, 2, dtype=F32) / AXIS_DIM)
    s = jnp.arange(seq)
    pos = jnp.where(jnp.asarray(axis)[:, None] == 0, (s // GRID_W)[None, :], (s % GRID_W)[None, :])
    ang = pos.astype(F32) * freqs[jnp.asarray(idx)][:, None]
    cos, sin = jnp.cos(ang), jnp.sin(ang)
    sign = jnp.where(jnp.asarray(lo), -1.0, 1.0)[:, None]

    def tables(gain, scale):
        g = gain.astype(F32) * scale
        return g[:, None] * cos, sign * g[jnp.asarray(partner)][:, None] * sin

    qa, qb = tables(q_gain, LOG2E * DH_A ** -0.5)
    ka, kb = tables(k_gain, 1.0)
    pad = lambda t: jnp.pad(t.T, ((0, 0), (0, LANES - DH_A)))
    lo_row = jnp.asarray(lo)[:, None]
    return qa, qb, pad(ka), pad(jnp.where(lo_row, kb, 0.0)), pad(jnp.where(lo_row, 0.0, kb))


def _prep_weights(w_in, b_if, norm_pre, w_mo, w_ao, w_out, norm_final):
    c0 = 5 * W_M
    c1 = c0 + 4 * N_HEADS_M
    c2 = c1 + W_A
    c3 = c2 + W_KV
    c4 = c3 + W_KV
    wb = w_in.astype(BF16)
    blk = np.arange(MXU_DIM) // LANES
    wk = wb[:, c2:c3].reshape(D_MODEL, N_KV_A, DH_A)
    return dict(
        w_m=jnp.concatenate([wb[:, :W_M], wb[:, 2 * W_M:c0]], axis=1),
        w_kt=wb[:, W_M:2 * W_M].T,
        w_if=jnp.pad(wb[:, c0:c1], ((0, 0), (0, LANES - 4 * N_HEADS_M))),
        w_qt=wb[:, c1:c2].T,
        w_kp=jnp.pad(wk, ((0, 0), (0, 0), (0, LANES - DH_A))).reshape(D_MODEL, N_KV_A * LANES),
        w_vt=wb[:, c3:c4].T,
        w_z=wb[:, c4:],
        bias=jnp.transpose(b_if.astype(F32), (2, 0, 1)).reshape(N_HEADS_M, 4),
        g_pre=norm_pre.astype(F32).reshape(1, D_MODEL),
        g_fin=norm_final.astype(F32).reshape(1, D_MODEL),
        ones=jnp.asarray(blk[:, None] == blk[None, :], dtype=BF16),
        w_mo=w_mo.astype(BF16), w_ao=w_ao.astype(BF16), w_out=w_out.astype(BF16),
    )


def _tiles(seq):
    tk = min(2 * MXU_DIM, seq)
    tq = tk if seq // tk >= 2 * ATT_TICKS + 2 else tk // 2
    return min(512, seq), min(4096, seq), tq, tk, min(MXU_DIM, seq)


def _layer(x, p, q_gain, k_gain):
    bsz, seq, _ = x.shape
    tm, sb, tq, tk, tf = _tiles(seq)
    x2 = x.reshape(bsz * seq, D_MODEL)

    q_m, kt, v_m, gate_m, if_raw = _mproj(x2, p["g_pre"], p["w_m"], p["w_kt"], p["w_if"], tm)
    qt, kc, vtc = _aproj(x, p["g_pre"], p["w_qt"], p["w_kp"], p["w_vt"], p["ones"],
                         *_rope_tables(seq, q_gain, k_gain), tk)

    shp = (bsz, seq, W_M)
    gates = if_raw[:, :4 * N_HEADS_M].reshape(bsz, seq // PAIR, PAIR, 4, N_HEADS_M)
    gates = jnp.transpose(gates, (0, 4, 3, 1, 2))
    h_f, h_b = _mlstm(p["bias"], q_m.reshape(shp), v_m.reshape(shp), kt, gates, sb)

    att = _attn(qt, kc, vtc, jnp.eye(tq, dtype=BF16), tq)

    y = _final(x2, h_f.reshape(-1, W_M), h_b.reshape(-1, W_M), gate_m, att.reshape(-1, W_A),
               p["g_pre"], p["g_fin"], p["w_z"], p["w_mo"], p["w_ao"], p["w_out"], tf)
    return y.reshape(bsz, seq, D_MODEL)


def kernel(x_prompt, x_sample, w_in, b_if, norm_pre, q_gain, k_gain, w_mo, w_ao, w_out, norm_final):
    assert w_in.shape[0] == 1, "single-layer model"
    p = _prep_weights(w_in[0], b_if[0], norm_pre[0], w_mo[0], w_ao[0], w_out[0], norm_final)
    return (_layer(x_prompt, p, q_gain[0], k_gain[0]), _layer(x_sample, p, q_gain[0], k_gain[0]))
```

```python
import functools

import numpy as np
import jax
import jax.numpy as jnp
from jax import lax
from jax.experimental import pallas as pl
from jax.experimental.pallas import tpu as pltpu

F32 = jnp.float32
BF16 = jnp.bfloat16

D_MODEL = 1024
N_HEADS_M = 4
DH_M = 256
W_M = N_HEADS_M * DH_M
CHUNK = 64
N_HEADS_A = 16
N_KV_A = 4
GROUP_A = N_HEADS_A // N_KV_A
DH_A = 64
W_A = N_HEADS_A * DH_A
W_KV = N_KV_A * DH_A
GRID_W = 64
AXIS_DIM = DH_A // 2
ROPE_THETA = 10000.0
EPS = 1e-6

LANES = 128
VT_ROWS = 80
MXU_DIM = 256
ATT_CHUNK = 128
ATT_TICKS = 6
LOG2E = 1.4426950408889634
PAIR = 2 * CHUNK
MLSTM_UNROLL = 2
VMEM_LIMIT = 56 * 1024 * 1024


def _cparams(sem):
    return pltpu.CompilerParams(dimension_semantics=sem, vmem_limit_bytes=VMEM_LIMIT)


def _rms_bf16(x, g):
    ms = jnp.mean(x * x, axis=-1, keepdims=True)
    return (x * lax.rsqrt(ms + EPS) * g).astype(BF16)


def _dot(a, b):
    return jnp.dot(a, b, preferred_element_type=F32)


def _dot_nt(a, b):
    return lax.dot_general(a, b, (((1,), (1,)), ((), ())), preferred_element_type=F32)


def _mproj_kernel(x_ref, g_ref, w_ref, wkt_ref, wif_ref, q_ref, kt_ref, v_ref, gate_ref, if_ref):
    tm = x_ref.shape[0]
    h = _rms_bf16(x_ref[...], g_ref[...])

    def proj(c):
        return _dot(h, w_ref[:, c * W_M:(c + 1) * W_M])

    q_ref[...] = proj(0).astype(BF16)
    v_ref[...] = proj(1).astype(BF16)
    o = proj(2)
    z = proj(3)
    gate_ref[...] = (jax.nn.sigmoid(o) * (z * jax.nn.sigmoid(z))).astype(BF16)
    if_ref[...] = _dot(h, wif_ref[...])
    kt = _dot_nt(wkt_ref[...], h) * (DH_M ** -0.5)
    for hh in range(N_HEADS_M):
        for p in range(tm // PAIR):
            kt_ref[hh, p] = kt[hh * DH_M:(hh + 1) * DH_M, p * PAIR:(p + 1) * PAIR].astype(BF16)


def _mproj(x2, g, w, wkt, wif, tm):
    t = x2.shape[0]
    tok = lambda i: (i, 0)
    fixed = lambda i: (0, 0)
    big = jax.ShapeDtypeStruct((t, W_M), BF16)
    return pl.pallas_call(
        _mproj_kernel,
        grid=(t // tm,),
        in_specs=[pl.BlockSpec((tm, D_MODEL), tok), pl.BlockSpec((1, D_MODEL), fixed),
                  pl.BlockSpec((D_MODEL, 4 * W_M), fixed), pl.BlockSpec((W_M, D_MODEL), fixed),
                  pl.BlockSpec((D_MODEL, LANES), fixed)],
        out_specs=[pl.BlockSpec((tm, W_M), tok),
                   pl.BlockSpec((N_HEADS_M, tm // PAIR, DH_M, PAIR), lambda i: (0, i, 0, 0)),
                   pl.BlockSpec((tm, W_M), tok), pl.BlockSpec((tm, W_M), tok), pl.BlockSpec((tm, LANES), tok)],
        out_shape=[big, jax.ShapeDtypeStruct((N_HEADS_M, t // PAIR, DH_M, PAIR), BF16), big, big,
                   jax.ShapeDtypeStruct((t, LANES), F32)],
        compiler_params=_cparams(("parallel",)),
        name="mproj",
    )(x2, g, w, wkt, wif)


def _aproj_kernel(x_ref, g_ref, wqt_ref, wkp_ref, wvt_ref, ones_ref, qa_ref, qb_ref, ka_ref, kup_ref, kdn_ref,
                  qt_ref, k_ref, vt_ref):
    tm = x_ref.shape[0]
    h = _rms_bf16(x_ref[...], g_ref[...])
    half = AXIS_DIM // 2

    qt = _dot_nt(wqt_ref[...], h).reshape(N_HEADS_A, DH_A, tm)
    xh = qt * lax.rsqrt(jnp.mean(qt * qt, axis=1, keepdims=True) + EPS)
    x5 = xh.reshape(N_HEADS_A, 2, 2, half, tm)
    partner = jnp.concatenate([x5[:, :, 1:2], x5[:, :, 0:1]], axis=2).reshape(N_HEADS_A, DH_A, tm)
    qt_ref[...] = (xh * qa_ref[...] + partner * qb_ref[...]).astype(BF16)

    kp = _dot(h, wkp_ref[...])
    ones = ones_ref[...]
    ka, kup, kdn = ka_ref[...], kup_ref[...], kdn_ref[...]
    for c in range(N_KV_A // 2):
        blk = kp[:, c * MXU_DIM:(c + 1) * MXU_DIM]
        ss = _dot((blk * blk).astype(BF16), ones)
        y = blk * lax.rsqrt(ss * (1.0 / DH_A) + EPS)
        for j in range(2):
            yc = y[:, j * LANES:(j + 1) * LANES]
            up = pltpu.roll(yc, LANES - half, 1)
            dn = pltpu.roll(yc, half, 1)
            k_ref[2 * c + j] = (yc * ka + up * kup + dn * kdn)[:, :DH_A].astype(BF16)

    vt = _dot_nt(wvt_ref[...], h)
    row = lax.broadcasted_iota(jnp.int32, (VT_ROWS - DH_A, tm), 0)
    extra = jnp.where(row == 0, 1.0, 0.0).astype(BF16)
    for hh in range(N_KV_A):
        vt_ref[hh, :DH_A, :] = vt[hh * DH_A:(hh + 1) * DH_A, :].astype(BF16)
        vt_ref[hh, DH_A:, :] = extra


def _aproj(x3, g, wqt, wkp, wvt, ones, qa, qb, ka, kup, kdn, tk):
    bsz, seq, _ = x3.shape
    nk = seq // tk
    fixed = lambda b, j: (0, 0)
    return pl.pallas_call(
        _aproj_kernel,
        grid=(bsz, nk),
        in_specs=[pl.BlockSpec((None, tk, D_MODEL), lambda b, j: (b, j, 0)), pl.BlockSpec((1, D_MODEL), fixed),
                  pl.BlockSpec((W_A, D_MODEL), fixed), pl.BlockSpec((D_MODEL, N_KV_A * LANES), fixed),
                  pl.BlockSpec((W_KV, D_MODEL), fixed), pl.BlockSpec((MXU_DIM, MXU_DIM), fixed),
                  pl.BlockSpec((DH_A, tk), lambda b, j: (0, j)), pl.BlockSpec((DH_A, tk), lambda b, j: (0, j)),
                  pl.BlockSpec((tk, LANES), lambda b, j: (j, 0)), pl.BlockSpec((tk, LANES), lambda b, j: (j, 0)),
                  pl.BlockSpec((tk, LANES), lambda b, j: (j, 0))],
        out_specs=[pl.BlockSpec((None, N_HEADS_A, DH_A, tk), lambda b, j: (b, 0, 0, j)),
                   pl.BlockSpec((None, N_KV_A, None, tk, DH_A), lambda b, j: (b, 0, j, 0, 0)),
                   pl.BlockSpec((None, N_KV_A, None, VT_ROWS, tk), lambda b, j: (b, 0, j, 0, 0))],
        out_shape=[jax.ShapeDtypeStruct((bsz, N_HEADS_A, DH_A, seq), BF16),
                   jax.ShapeDtypeStruct((bsz, N_KV_A, nk, tk, DH_A), BF16),
                   jax.ShapeDtypeStruct((bsz, N_KV_A, nk, VT_ROWS, tk), BF16)],
        compiler_params=_cparams(("parallel", "parallel")),
        name="aproj",
    )(x3, g, wqt, wkp, wvt, ones, qa, qb, ka, kup, kdn)


def _split3(x):
    h1 = x.astype(BF16)
    r1 = x - h1.astype(F32)
    h2 = r1.astype(BF16)
    return h1, h2, (r1 - h2.astype(F32)).astype(BF16)


def _mlstm_kernel(bias_ref,
                  qf_ref, vf_ref, ktf_ref, gf_ref,
                  qb_ref, vb_ref, ktb_ref, gb_ref,
                  hf_ref, hb_ref,
                  cta_scr, m_scr, a_scr, lf_scr, mc_scr, *, npb):
    head = pl.program_id(1)

    @pl.when(pl.program_id(2) == 0)
    def _():
        cta_scr[...] = jnp.zeros_like(cta_scr)
        m_scr[...] = jnp.zeros_like(m_scr)

    ti = lax.broadcasted_iota(jnp.int32, (CHUNK, PAIR), 0)
    li = lax.broadcasted_iota(jnp.int32, (CHUNK, PAIR), 1)
    ri = lax.broadcasted_iota(jnp.int32, (PAIR, PAIR), 0)
    ci = lax.broadcasted_iota(jnp.int32, (PAIR, PAIR), 1)
    same_chunk = (ri >= CHUNK) == (ci >= CHUNK)
    low_half = lax.broadcasted_iota(jnp.int32, (npb, PAIR), 1) < CHUNK
    ones_b = jnp.ones((PAIR, LANES), BF16)

    def valid_mask(d, par):
        rel = li - CHUNK * par
        return (rel >= ti) & (rel < CHUNK) if d else (rel <= ti) & (rel >= 0)

    for d, g_ref in ((0, gf_ref), (1, gb_ref)):
        last = 0 if d else CHUNK - 1
        xf = g_ref[2 * d + 1] + bias_ref[head, 2 * d + 1]
        lf = (jnp.minimum(xf, 0.0) - jnp.log(1.0 + jnp.exp(-jnp.abs(xf)))) * LOG2E
        cmat = jnp.where(same_chunk & ((ri >= ci) if d else (ri <= ci)), 1.0, 0.0).astype(BF16)
        b = sum(_dot(t, cmat) for t in _split3(lf))
        a = (g_ref[2 * d] + bias_ref[head, 2 * d]) * LOG2E - b
        a_scr[d] = a
        lf_scr[d] = lf
        amax = [jnp.broadcast_to(jnp.max(jnp.where(low_half == (par == 0), a, -jnp.inf), axis=1, keepdims=True),
                                 (npb, LANES)) for par in (0, 1)]
        blast = [jnp.broadcast_to(b[:, CHUNK * par + last:CHUNK * par + last + 1], (npb, LANES)) for par in (0, 1)]
        m = m_scr[d:d + 1, :]
        for p in (range(npb - 1, -1, -1) if d else range(npb)):
            for par in ((1, 0) if d else (0, 1)):
                c = 2 * p + par
                mc_scr[d, c:c + 1, :] = m
                m = blast[par][p:p + 1, :] + jnp.maximum(m, amax[par][p:p + 1, :])
        m_scr[d:d + 1, :] = m

    dirs = ((qf_ref, vf_ref, ktf_ref, hf_ref), (qb_ref, vb_ref, ktb_ref, hb_ref))

    def rows_of(c):
        return pl.ds(pl.multiple_of(c * CHUNK, CHUNK), CHUNK)

    def stage1(d, p, par):
        q_ref, _, kt_ref, _ = dirs[d]
        c = 2 * p + par
        valid = valid_mask(d, par)
        last = 0 if d else CHUNK - 1
        a_row = a_scr[d, pl.ds(p, 1), :]
        lf_row = lf_scr[d, pl.ds(p, 1), :]
        m = mc_scr[d, pl.ds(c, 1), 0:1]
        cm = jnp.max(jnp.where(valid, a_row, -jnp.inf), axis=1, keepdims=True)
        bcol = jnp.sum(jnp.where(valid, lf_row, 0.0), axis=1, keepdims=True)
        mc = jnp.maximum(cm, m)
        dmat = jnp.where(valid, jnp.exp2(a_row - mc), 0.0)
        inter = jnp.broadcast_to(jnp.exp2(m - mc), (CHUNK, LANES))
        emt = jnp.broadcast_to(jnp.exp2(-(bcol + mc)), (CHUNK, LANES))
        dec = jnp.broadcast_to(jnp.exp2(m - mc[last:last + 1]), (1, LANES))
        sd = (_dot(q_ref[rows_of(c), :], kt_ref[p]) * dmat).astype(BF16)
        return sd, dmat[last:last + 1, :], inter, emt, dec

    def stage2(d, p, par, pre):
        sd, ws_row, inter, emt, dec = pre
        q_ref, v_ref, kt_ref, h_ref = dirs[d]
        rows = rows_of(2 * p + par)
        q = q_ref[rows, :]
        v_pair = v_ref[pl.ds(pl.multiple_of(p * PAIR, PAIR), PAIR), :]
        v_aug = jnp.concatenate([v_pair, ones_b], axis=1)
        cta = cta_scr[d]
        ktw = (kt_ref[p].astype(F32) * ws_row).astype(BF16)
        both = _dot(jnp.concatenate([sd, ktw], axis=0), v_aug)
        na = both[:CHUNK] + jnp.concatenate([inter, inter, inter], axis=1) * _dot(q, cta.astype(BF16))
        r = 1.0 / jnp.maximum(jnp.abs(na[:, DH_M:]), emt)
        h_ref[rows, :] = (na[:, :DH_M] * jnp.concatenate([r, r], axis=1)).astype(BF16)
        cta_scr[d] = jnp.concatenate([dec, dec, dec], axis=1) * cta + both[CHUNK:]

    def body(i, carry):
        pf = i
        pb = npb - 1 - i
        mid_f = stage1(0, pf, 1)
        mid_b = stage1(1, pb, 0)
        stage2(0, pf, 0, carry[0])
        stage2(1, pb, 1, carry[1])
        nxt_f = stage1(0, jnp.minimum(pf + 1, npb - 1), 0)
        nxt_b = stage1(1, jnp.maximum(pb - 1, 0), 1)
        stage2(0, pf, 1, mid_f)
        stage2(1, pb, 0, mid_b)
        return nxt_f, nxt_b

    lax.fori_loop(0, npb, body, (stage1(0, 0, 0), stage1(1, npb - 1, 1)), unroll=MLSTM_UNROLL)


def _mlstm(bias, q, v, kt, gates, sb):
    bsz, seq, _ = q.shape
    nb = seq // sb
    npb = sb // PAIR
    fwd = lambda b, h, j: (b, j, h)
    bwd = lambda b, h, j: (b, nb - 1 - j, h)
    kfwd = lambda b, h, j: (h, b * nb + j, 0, 0)
    kbwd = lambda b, h, j: (h, b * nb + nb - 1 - j, 0, 0)
    gfwd = lambda b, h, j: (b, h, 0, j, 0)
    gbwd = lambda b, h, j: (b, h, 0, nb - 1 - j, 0)
    tok = lambda im: pl.BlockSpec((None, sb, DH_M), im)
    ktb = lambda im: pl.BlockSpec((None, npb, DH_M, PAIR), im)
    gb = lambda im: pl.BlockSpec((None, None, 4, npb, PAIR), im)
    out = jax.ShapeDtypeStruct((bsz, seq, W_M), BF16)
    return pl.pallas_call(
        functools.partial(_mlstm_kernel, npb=npb),
        grid=(bsz, N_HEADS_M, nb),
        in_specs=[pl.BlockSpec(memory_space=pltpu.SMEM),
                  tok(fwd), tok(fwd), ktb(kfwd), gb(gfwd),
                  tok(bwd), tok(bwd), ktb(kbwd), gb(gbwd)],
        out_specs=[tok(fwd), tok(bwd)],
        out_shape=[out, out],
        scratch_shapes=[pltpu.VMEM((2, DH_M, DH_M + LANES), F32), pltpu.VMEM((8, LANES), F32),
                        pltpu.VMEM((2, npb, PAIR), F32), pltpu.VMEM((2, npb, PAIR), F32),
                        pltpu.VMEM((2, 2 * npb, LANES), F32)],
        compiler_params=_cparams(("parallel", "parallel", "arbitrary")),
        name="mlstm",
    )(bias, q, v, kt, gates, q, v, kt, gates)


def _attn_kernel(qt_ref, k_ref, vt_ref, o_ref, st0, st1, *, nk):
    st_scr = (st0, st1)
    tq = qt_ref.shape[-1]
    tk = k_ref.shape[1]
    qs = [qt_ref[g] for g in range(GROUP_A)]

    def tick(t, slot, do_qk, do_sm, state):
        pmax, ms, accs = state
        n_pmax, n_ms, n_accs = list(pmax), list(ms), list(accs)
        for g in range(GROUP_A):
            if do_sm:
                m_new = jnp.maximum(ms[g], jnp.max(pmax[g], axis=0, keepdims=True))
                acc = jnp.exp2(ms[g] - m_new) * accs[g]
                n_ms[g] = m_new
            pm = None
            pts = []
            for c in range(tk // ATT_CHUNK):
                rows = slice(c * ATT_CHUNK, (c + 1) * ATT_CHUNK)
                if do_qk:
                    st = _dot(k_ref[t, rows, :], qs[g])
                    st_scr[slot][g, rows, :] = st
                    cm = jnp.max(st.reshape(ATT_CHUNK // 8, 8, tq), axis=0)
                    pm = cm if pm is None else jnp.maximum(pm, cm)
                if do_sm:
                    pts.append(jnp.exp2(st_scr[1 - slot][g, rows, :] - m_new).astype(BF16))
                    if (c + 1) * ATT_CHUNK % MXU_DIM == 0:
                        kt = (c + 1) * ATT_CHUNK // MXU_DIM - 1
                        cols = slice(kt * MXU_DIM, (kt + 1) * MXU_DIM)
                        acc = acc + _dot(vt_ref[t - 1, :, cols], jnp.concatenate(pts, axis=0))
                        pts = []
            if do_qk:
                n_pmax[g] = pm
            if do_sm:
                n_accs[g] = acc
        return tuple(n_pmax), tuple(n_ms), tuple(n_accs)

    neg = tuple(jnp.full((1, tq), -jnp.inf, F32) for _ in range(GROUP_A))
    neg8 = tuple(jnp.full((8, tq), -jnp.inf, F32) for _ in range(GROUP_A))
    state = (neg8, neg, tuple(jnp.zeros((VT_ROWS, tq), F32) for _ in range(GROUP_A)))
    state = tick(0, 0, True, False, state)
    state = tick(1, 1, True, True, state)
    ntrip = (nk - 2) // ATT_TICKS

    def body(i, state):
        for u in range(ATT_TICKS):
            state = tick(ATT_TICKS * i + 2 + u, u % 2, True, True, state)
        return state

    state = lax.fori_loop(0, ntrip, body, state)
    for t in range(2 + ntrip * ATT_TICKS, nk):
        state = tick(t, t % 2, True, True, state)
    state = tick(nk, nk % 2, False, True, state)
    ot = jnp.concatenate([acc[:DH_A] * (1.0 / acc[DH_A:DH_A + 1]) for acc in state[2]], axis=0)
    o_ref[...] = ot.T.astype(BF16)


def _attn(qt, kc, vtc, tq):
    bsz, _, _, seq = qt.shape
    nk, tk = kc.shape[2], kc.shape[3]
    assert nk >= 2 and tk % MXU_DIM == 0
    return pl.pallas_call(
        functools.partial(_attn_kernel, nk=nk),
        grid=(bsz, N_KV_A, seq // tq),
        in_specs=[pl.BlockSpec((None, GROUP_A, DH_A, tq), lambda b, h, i: (b, h, 0, i)),
                  pl.BlockSpec((None, None, nk, tk, DH_A), lambda b, h, i: (b, h, 0, 0, 0)),
                  pl.BlockSpec((None, None, nk, VT_ROWS, tk), lambda b, h, i: (b, h, 0, 0, 0))],
        out_specs=pl.BlockSpec((None, tq, GROUP_A * DH_A), lambda b, h, i: (b, i, h)),
        out_shape=jax.ShapeDtypeStruct((bsz, seq, W_A), BF16),
        scratch_shapes=[pltpu.VMEM((GROUP_A, tk, tq), F32), pltpu.VMEM((GROUP_A, tk, tq), F32)],
        compiler_params=_cparams(("parallel", "parallel", "arbitrary")),
        name="attn",
    )(qt, kc, vtc)


def _final_kernel(x_ref, hf_ref, hb_ref, gm_ref, att_ref, gpre_ref, gfin_ref,
                  wz_ref, wmo_ref, wao_ref, wout_ref, y_ref):
    x = x_ref[...]
    h = _rms_bf16(x, gpre_ref[...])
    z_a = _dot(h, wz_ref[:, :W_A])
    um = ((hf_ref[...].astype(F32) + hb_ref[...].astype(F32)) * gm_ref[...].astype(F32)).astype(BF16)
    ua = (att_ref[...].astype(F32) * (z_a * jax.nn.sigmoid(z_a))).astype(BF16)
    y_m = _dot(um, wmo_ref[...])
    y_a = _dot(ua, wao_ref[...])
    g_m = _dot(h, wz_ref[:, W_A:W_A + D_MODEL])
    g_a = _dot(h, wz_ref[:, W_A + D_MODEL:])
    merged = jax.nn.sigmoid(g_m) * y_m + jax.nn.sigmoid(g_a) * y_a
    out = x + _dot(merged.astype(BF16), wout_ref[...])
    ms = jnp.mean(out * out, axis=-1, keepdims=True)
    y_ref[...] = out * lax.rsqrt(ms + EPS) * gfin_ref[...]


def _final(x2, hf, hb, gm, att, gpre, gfin, wz, wmo, wao, wout, tm):
    t = x2.shape[0]
    tok = lambda i: (i, 0)
    fixed = lambda i: (0, 0)
    act = pl.BlockSpec((tm, D_MODEL), tok)
    sq = pl.BlockSpec((D_MODEL, D_MODEL), fixed)
    vec = pl.BlockSpec((1, D_MODEL), fixed)
    return pl.pallas_call(
        _final_kernel,
        grid=(t // tm,),
        in_specs=[act, act, act, act, act, vec, vec, pl.BlockSpec((D_MODEL, 3 * D_MODEL), fixed), sq, sq, sq],
        out_specs=act,
        out_shape=jax.ShapeDtypeStruct((t, D_MODEL), F32),
        compiler_params=_cparams(("parallel",)),
        name="final",
    )(x2, hf, hb, gm, att, gpre, gfin, wz, wmo, wao, wout)


def _rope_tables(seq, q_gain, k_gain):
    half = AXIS_DIM // 2
    d = np.arange(DH_A)
    axis = d // AXIS_DIM
    lo = (d % AXIS_DIM) < half
    idx = (d % AXIS_DIM) % half
    partner = np.where(lo, d + half, d - half)
    freqs = ROPE_THETA ** (-jnp.arange(0, AXIS_DIM, 2, dtype=F32) / AXIS_DIM)
    s = jnp.arange(seq)
    pos = jnp.where(jnp.asarray(axis)[:, None] == 0, (s // GRID_W)[None, :], (s % GRID_W)[None, :])
    ang = pos.astype(F32) * freqs[jnp.asarray(idx)][:, None]
    cos, sin = jnp.cos(ang), jnp.sin(ang)
    sign = jnp.where(jnp.asarray(lo), -1.0, 1.0)[:, None]

    def tables(gain, scale):
        g = gain.astype(F32) * scale
        return g[:, None] * cos, sign * g[jnp.asarray(partner)][:, None] * sin

    qa, qb = tables(q_gain, LOG2E * DH_A ** -0.5)
    ka, kb = tables(k_gain, 1.0)
    pad = lambda t: jnp.pad(t.T, ((0, 0), (0, LANES - DH_A)))
    lo_row = jnp.asarray(lo)[:, None]
    return qa, qb, pad(ka), pad(jnp.where(lo_row, kb, 0.0)), pad(jnp.where(lo_row, 0.0, kb))


def _prep_weights(w_in, b_if, norm_pre, w_mo, w_ao, w_out, norm_final):
    c0 = 5 * W_M
    c1 = c0 + 4 * N_HEADS_M
    c2 = c1 + W_A
    c3 = c2 + W_KV
    c4 = c3 + W_KV
    wb = w_in.astype(BF16)
    blk = np.arange(MXU_DIM) // LANES
    wk = wb[:, c2:c3].reshape(D_MODEL, N_KV_A, DH_A)
    return dict(
        w_m=jnp.concatenate([wb[:, :W_M], wb[:, 2 * W_M:c0]], axis=1),
        w_kt=wb[:, W_M:2 * W_M].T,
        w_if=jnp.pad(wb[:, c0:c1], ((0, 0), (0, LANES - 4 * N_HEADS_M))),
        w_qt=wb[:, c1:c2].T,
        w_kp=jnp.pad(wk, ((0, 0), (0, 0), (0, LANES - DH_A))).reshape(D_MODEL, N_KV_A * LANES),
        w_vt=wb[:, c3:c4].T,
        w_z=wb[:, c4:],
        bias=jnp.transpose(b_if.astype(F32), (2, 0, 1)).reshape(N_HEADS_M, 4),
        g_pre=norm_pre.astype(F32).reshape(1, D_MODEL),
        g_fin=norm_final.astype(F32).reshape(1, D_MODEL),
        ones=jnp.asarray(blk[:, None] == blk[None, :], dtype=BF16),
        w_mo=w_mo.astype(BF16), w_ao=w_ao.astype(BF16), w_out=w_out.astype(BF16),
    )


def _tiles(seq):
    tk = min(2 * MXU_DIM, seq)
    tq = tk if seq // tk >= 2 * ATT_TICKS + 2 else tk // 2
    return min(512, seq), min(4096, seq), tq, tk, min(MXU_DIM, seq)


def _layer(x, p, q_gain, k_gain):
    bsz, seq, _ = x.shape
    tm, sb, tq, tk, tf = _tiles(seq)
    x2 = x.reshape(bsz * seq, D_MODEL)

    q_m, kt, v_m, gate_m, if_raw = _mproj(x2, p["g_pre"], p["w_m"], p["w_kt"], p["w_if"], tm)
    qt, kc, vtc = _aproj(x, p["g_pre"], p["w_qt"], p["w_kp"], p["w_vt"], p["ones"],
                         *_rope_tables(seq, q_gain, k_gain), tk)

    shp = (bsz, seq, W_M)
    gates = if_raw[:, :4 * N_HEADS_M].reshape(bsz, seq // PAIR, PAIR, 4, N_HEADS_M)
    gates = jnp.transpose(gates, (0, 4, 3, 1, 2))
    h_f, h_b = _mlstm(p["bias"], q_m.reshape(shp), v_m.reshape(shp), kt, gates, sb)

    att = _attn(qt, kc, vtc, tq)

    y = _final(x2, h_f.reshape(-1, W_M), h_b.reshape(-1, W_M), gate_m, att.reshape(-1, W_A),
               p["g_pre"], p["g_fin"], p["w_z"], p["w_mo"], p["w_ao"], p["w_out"], tf)
    return y.reshape(bsz, seq, D_MODEL)


def kernel(x_prompt, x_sample, w_in, b_if, norm_pre, q_gain, k_gain, w_mo, w_ao, w_out, norm_final):
    assert w_in.shape[0] == 1, "single-layer model"
    p = _prep_weights(w_in[0], b_if[0], norm_pre[0], w_mo[0], w_ao[0], w_out[0], norm_final)
    return (_layer(x_prompt, p, q_gain[0], k_gain[0]), _layer(x_sample, p, q_gain[0], k_gain[0]))
```

```python
import functools

import numpy as np
import jax
import jax.numpy as jnp
from jax import lax
from jax.experimental import pallas as pl
from jax.experimental.pallas import tpu as pltpu

F32 = jnp.float32
BF16 = jnp.bfloat16

D_MODEL = 1024
N_HEADS_M = 4
DH_M = 256
W_M = N_HEADS_M * DH_M
CHUNK = 64
N_HEADS_A = 16
N_KV_A = 4
GROUP_A = N_HEADS_A // N_KV_A
DH_A = 64
W_A = N_HEADS_A * DH_A
W_KV = N_KV_A * DH_A
GRID_W = 64
AXIS_DIM = DH_A // 2
ROPE_THETA = 10000.0
EPS = 1e-6

LANES = 128
VT_ROWS = 80
MXU_DIM = 256
ATT_CHUNK = 128
ATT_TICKS = 6
LOG2E = 1.4426950408889634
PAIR = 2 * CHUNK
MLSTM_UNROLL = 2
VMEM_LIMIT = 56 * 1024 * 1024


def _cparams(sem):
    return pltpu.CompilerParams(dimension_semantics=sem, vmem_limit_bytes=VMEM_LIMIT)


def _rms_bf16(x, g):
    ms = jnp.mean(x * x, axis=-1, keepdims=True)
    return (x * lax.rsqrt(ms + EPS) * g).astype(BF16)


def _dot(a, b):
    return jnp.dot(a, b, preferred_element_type=F32)


def _dot_nt(a, b):
    return lax.dot_general(a, b, (((1,), (1,)), ((), ())), preferred_element_type=F32)


def _mproj_kernel(x_ref, g_ref, w_ref, wkt_ref, wif_ref, q_ref, kt_ref, v_ref, gate_ref, if_ref):
    tm = x_ref.shape[0]
    h = _rms_bf16(x_ref[...], g_ref[...])

    def proj(c):
        return _dot(h, w_ref[:, c * W_M:(c + 1) * W_M])

    q_ref[...] = proj(0).astype(BF16)
    v_ref[...] = proj(1).astype(BF16)
    o = proj(2)
    z = proj(3)
    gate_ref[...] = (jax.nn.sigmoid(o) * (z * jax.nn.sigmoid(z))).astype(BF16)
    if_ref[...] = _dot(h, wif_ref[...])
    kt = _dot_nt(wkt_ref[...], h) * (DH_M ** -0.5)
    for hh in range(N_HEADS_M):
        for p in range(tm // PAIR):
            kt_ref[hh, p] = kt[hh * DH_M:(hh + 1) * DH_M, p * PAIR:(p + 1) * PAIR].astype(BF16)


def _mproj(x2, g, w, wkt, wif, tm):
    t = x2.shape[0]
    tok = lambda i: (i, 0)
    fixed = lambda i: (0, 0)
    big = jax.ShapeDtypeStruct((t, W_M), BF16)
    return pl.pallas_call(
        _mproj_kernel,
        grid=(t // tm,),
        in_specs=[pl.BlockSpec((tm, D_MODEL), tok), pl.BlockSpec((1, D_MODEL), fixed),
                  pl.BlockSpec((D_MODEL, 4 * W_M), fixed), pl.BlockSpec((W_M, D_MODEL), fixed),
                  pl.BlockSpec((D_MODEL, LANES), fixed)],
        out_specs=[pl.BlockSpec((tm, W_M), tok),
                   pl.BlockSpec((N_HEADS_M, tm // PAIR, DH_M, PAIR), lambda i: (0, i, 0, 0)),
                   pl.BlockSpec((tm, W_M), tok), pl.BlockSpec((tm, W_M), tok), pl.BlockSpec((tm, LANES), tok)],
        out_shape=[big, jax.ShapeDtypeStruct((N_HEADS_M, t // PAIR, DH_M, PAIR), BF16), big, big,
                   jax.ShapeDtypeStruct((t, LANES), F32)],
        compiler_params=_cparams(("parallel",)),
        name="mproj",
    )(x2, g, w, wkt, wif)


def _aproj_kernel(x_ref, g_ref, wqt_ref, wkt_ref, wvt_ref, qa_ref, qb_ref, ka_ref, kb_ref,
                  qt_ref, k_ref, vt_ref):
    tm = x_ref.shape[0]
    h = _rms_bf16(x_ref[...], g_ref[...])
    half = AXIS_DIM // 2

    def norm_rope(w_ref, a_ref, b_ref, nh):
        t = _dot_nt(w_ref[...], h).reshape(nh, DH_A, tm)
        xh = t * lax.rsqrt(jnp.mean(t * t, axis=1, keepdims=True) + EPS)
        x5 = xh.reshape(nh, 2, 2, half, tm)
        partner = jnp.concatenate([x5[:, :, 1:2], x5[:, :, 0:1]], axis=2).reshape(nh, DH_A, tm)
        return xh * a_ref[...] + partner * b_ref[...]

    qt_ref[...] = norm_rope(wqt_ref, qa_ref, qb_ref, N_HEADS_A).astype(BF16)

    ko = norm_rope(wkt_ref, ka_ref, kb_ref, N_KV_A)
    kpad = jnp.concatenate([ko, jnp.zeros_like(ko)], axis=1).reshape(N_KV_A * LANES, tm)
    knat = kpad.T
    for hh in range(N_KV_A):
        k_ref[hh] = knat[:, hh * LANES:hh * LANES + DH_A].astype(BF16)

    vt = _dot_nt(wvt_ref[...], h)
    row = lax.broadcasted_iota(jnp.int32, (VT_ROWS - DH_A, tm), 0)
    extra = jnp.where(row == 0, 1.0, 0.0).astype(BF16)
    for hh in range(N_KV_A):
        vt_ref[hh, :DH_A, :] = vt[hh * DH_A:(hh + 1) * DH_A, :].astype(BF16)
        vt_ref[hh, DH_A:, :] = extra


def _aproj(x3, g, wqt, wkt, wvt, qa, qb, ka, kb, tk):
    bsz, seq, _ = x3.shape
    nk = seq // tk
    fixed = lambda b, j: (0, 0)
    tab = pl.BlockSpec((DH_A, tk), lambda b, j: (0, j))
    return pl.pallas_call(
        _aproj_kernel,
        grid=(bsz, nk),
        in_specs=[pl.BlockSpec((None, tk, D_MODEL), lambda b, j: (b, j, 0)), pl.BlockSpec((1, D_MODEL), fixed),
                  pl.BlockSpec((W_A, D_MODEL), fixed), pl.BlockSpec((W_KV, D_MODEL), fixed),
                  pl.BlockSpec((W_KV, D_MODEL), fixed), tab, tab, tab, tab],
        out_specs=[pl.BlockSpec((None, N_HEADS_A, DH_A, tk), lambda b, j: (b, 0, 0, j)),
                   pl.BlockSpec((None, N_KV_A, None, tk, DH_A), lambda b, j: (b, 0, j, 0, 0)),
                   pl.BlockSpec((None, N_KV_A, None, VT_ROWS, tk), lambda b, j: (b, 0, j, 0, 0))],
        out_shape=[jax.ShapeDtypeStruct((bsz, N_HEADS_A, DH_A, seq), BF16),
                   jax.ShapeDtypeStruct((bsz, N_KV_A, nk, tk, DH_A), BF16),
                   jax.ShapeDtypeStruct((bsz, N_KV_A, nk, VT_ROWS, tk), BF16)],
        compiler_params=_cparams(("parallel", "parallel")),
        name="aproj",
    )(x3, g, wqt, wkt, wvt, qa, qb, ka, kb)


def _split3(x):
    h1 = x.astype(BF16)
    r1 = x - h1.astype(F32)
    h2 = r1.astype(BF16)
    return h1, h2, (r1 - h2.astype(F32)).astype(BF16)


def _mlstm_kernel(bias_ref,
                  qf_ref, vf_ref, ktf_ref, gf_ref,
                  qb_ref, vb_ref, ktb_ref, gb_ref,
                  hf_ref, hb_ref,
                  cta_scr, m_scr, a_scr, lf_scr, mc_scr, *, npb):
    head = pl.program_id(1)

    @pl.when(pl.program_id(2) == 0)
    def _():
        cta_scr[...] = jnp.zeros_like(cta_scr)
        m_scr[...] = jnp.zeros_like(m_scr)

    ti = lax.broadcasted_iota(jnp.int32, (CHUNK, PAIR), 0)
    li = lax.broadcasted_iota(jnp.int32, (CHUNK, PAIR), 1)
    ri = lax.broadcasted_iota(jnp.int32, (PAIR, PAIR), 0)
    ci = lax.broadcasted_iota(jnp.int32, (PAIR, PAIR), 1)
    same_chunk = (ri >= CHUNK) == (ci >= CHUNK)
    low_half = lax.broadcasted_iota(jnp.int32, (npb, PAIR), 1) < CHUNK
    ones_b = jnp.ones((PAIR, LANES), BF16)

    def valid_mask(d, par):
        rel = li - CHUNK * par
        return (rel >= ti) & (rel < CHUNK) if d else (rel <= ti) & (rel >= 0)

    for d, g_ref in ((0, gf_ref), (1, gb_ref)):
        last = 0 if d else CHUNK - 1
        xf = g_ref[2 * d + 1] + bias_ref[head, 2 * d + 1]
        lf = (jnp.minimum(xf, 0.0) - jnp.log(1.0 + jnp.exp(-jnp.abs(xf)))) * LOG2E
        cmat = jnp.where(same_chunk & ((ri >= ci) if d else (ri <= ci)), 1.0, 0.0).astype(BF16)
        b = sum(_dot(t, cmat) for t in _split3(lf))
        a = (g_ref[2 * d] + bias_ref[head, 2 * d]) * LOG2E - b
        a_scr[d] = a
        lf_scr[d] = lf
        amax = [jnp.broadcast_to(jnp.max(jnp.where(low_half == (par == 0), a, -jnp.inf), axis=1, keepdims=True),
                                 (npb, LANES)) for par in (0, 1)]
        blast = [jnp.broadcast_to(b[:, CHUNK * par + last:CHUNK * par + last + 1], (npb, LANES)) for par in (0, 1)]
        m = m_scr[d:d + 1, :]
        for p in (range(npb - 1, -1, -1) if d else range(npb)):
            for par in ((1, 0) if d else (0, 1)):
                c = 2 * p + par
                mc_scr[d, c:c + 1, :] = m
                m = blast[par][p:p + 1, :] + jnp.maximum(m, amax[par][p:p + 1, :])
        m_scr[d:d + 1, :] = m

    dirs = ((qf_ref, vf_ref, ktf_ref, hf_ref), (qb_ref, vb_ref, ktb_ref, hb_ref))

    def rows_of(c):
        return pl.ds(pl.multiple_of(c * CHUNK, CHUNK), CHUNK)

    def stage1(d, p, par):
        q_ref, _, kt_ref, _ = dirs[d]
        c = 2 * p + par
        valid = valid_mask(d, par)
        last = 0 if d else CHUNK - 1
        a_row = a_scr[d, pl.ds(p, 1), :]
        lf_row = lf_scr[d, pl.ds(p, 1), :]
        m = mc_scr[d, pl.ds(c, 1), 0:1]
        cm = jnp.max(jnp.where(valid, a_row, -jnp.inf), axis=1, keepdims=True)
        bcol = jnp.sum(jnp.where(valid, lf_row, 0.0), axis=1, keepdims=True)
        mc = jnp.maximum(cm, m)
        dmat = jnp.where(valid, jnp.exp2(a_row - mc), 0.0)
        inter = jnp.broadcast_to(jnp.exp2(m - mc), (CHUNK, LANES))
        emt = jnp.broadcast_to(jnp.exp2(-(bcol + mc)), (CHUNK, LANES))
        dec = jnp.broadcast_to(jnp.exp2(m - mc[last:last + 1]), (1, LANES))
        sd = (_dot(q_ref[rows_of(c), :], kt_ref[p]) * dmat).astype(BF16)
        return sd, dmat[last:last + 1, :], inter, emt, dec

    def stage2(d, p, par, pre):
        sd, ws_row, inter, emt, dec = pre
        q_ref, v_ref, kt_ref, h_ref = dirs[d]
        rows = rows_of(2 * p + par)
        q = q_ref[rows, :]
        v_pair = v_ref[pl.ds(pl.multiple_of(p * PAIR, PAIR), PAIR), :]
        v_aug = jnp.concatenate([v_pair, ones_b], axis=1)
        cta = cta_scr[d]
        ktw = (kt_ref[p].astype(F32) * ws_row).astype(BF16)
        both = _dot(jnp.concatenate([sd, ktw], axis=0), v_aug)
        na = both[:CHUNK] + jnp.concatenate([inter, inter, inter], axis=1) * _dot(q, cta.astype(BF16))
        r = 1.0 / jnp.maximum(jnp.abs(na[:, DH_M:]), emt)
        h_ref[rows, :] = (na[:, :DH_M] * jnp.concatenate([r, r], axis=1)).astype(BF16)
        cta_scr[d] = jnp.concatenate([dec, dec, dec], axis=1) * cta + both[CHUNK:]

    def body(i, carry):
        pf = i
        pb = npb - 1 - i
        mid_f = stage1(0, pf, 1)
        mid_b = stage1(1, pb, 0)
        stage2(0, pf, 0, carry[0])
        stage2(1, pb, 1, carry[1])
        nxt_f = stage1(0, jnp.minimum(pf + 1, npb - 1), 0)
        nxt_b = stage1(1, jnp.maximum(pb - 1, 0), 1)
        stage2(0, pf, 1, mid_f)
        stage2(1, pb, 0, mid_b)
        return nxt_f, nxt_b

    lax.fori_loop(0, npb, body, (stage1(0, 0, 0), stage1(1, npb - 1, 1)), unroll=MLSTM_UNROLL)


def _mlstm(bias, q, v, kt, gates, sb):
    bsz, seq, _ = q.shape
    nb = seq // sb
    npb = sb // PAIR
    fwd = lambda b, h, j: (b, j, h)
    bwd = lambda b, h, j: (b, nb - 1 - j, h)
    kfwd = lambda b, h, j: (h, b * nb + j, 0, 0)
    kbwd = lambda b, h, j: (h, b * nb + nb - 1 - j, 0, 0)
    gfwd = lambda b, h, j: (b, h, 0, j, 0)
    gbwd = lambda b, h, j: (b, h, 0, nb - 1 - j, 0)
    tok = lambda im: pl.BlockSpec((None, sb, DH_M), im)
    ktb = lambda im: pl.BlockSpec((None, npb, DH_M, PAIR), im)
    gb = lambda im: pl.BlockSpec((None, None, 4, npb, PAIR), im)
    out = jax.ShapeDtypeStruct((bsz, seq, W_M), BF16)
    return pl.pallas_call(
        functools.partial(_mlstm_kernel, npb=npb),
        grid=(bsz, N_HEADS_M, nb),
        in_specs=[pl.BlockSpec(memory_space=pltpu.SMEM),
                  tok(fwd), tok(fwd), ktb(kfwd), gb(gfwd),
                  tok(bwd), tok(bwd), ktb(kbwd), gb(gbwd)],
        out_specs=[tok(fwd), tok(bwd)],
        out_shape=[out, out],
        scratch_shapes=[pltpu.VMEM((2, DH_M, DH_M + LANES), F32), pltpu.VMEM((8, LANES), F32),
                        pltpu.VMEM((2, npb, PAIR), F32), pltpu.VMEM((2, npb, PAIR), F32),
                        pltpu.VMEM((2, 2 * npb, LANES), F32)],
        compiler_params=_cparams(("parallel", "parallel", "arbitrary")),
        name="mlstm",
    )(bias, q, v, kt, gates, q, v, kt, gates)


def _attn_kernel(qt_ref, k_ref, vt_ref, o_ref, st0, st1, *, nk):
    st_scr = (st0, st1)
    tq = qt_ref.shape[-1]
    tk = k_ref.shape[1]
    qs = [qt_ref[g] for g in range(GROUP_A)]

    def tick(t, slot, do_qk, do_sm, state):
        pmax, ms, accs = state
        n_pmax, n_ms, n_accs = list(pmax), list(ms), list(accs)
        for g in range(GROUP_A):
            if do_sm:
                m_new = jnp.maximum(ms[g], jnp.max(pmax[g], axis=0, keepdims=True))
                acc = jnp.exp2(ms[g] - m_new) * accs[g]
                n_ms[g] = m_new
            pm = None
            pts = []
            for c in range(tk // ATT_CHUNK):
                rows = slice(c * ATT_CHUNK, (c + 1) * ATT_CHUNK)
                if do_qk:
                    st = _dot(k_ref[t, rows, :], qs[g])
                    st_scr[slot][g, rows, :] = st
                    cm = jnp.max(st.reshape(ATT_CHUNK // 8, 8, tq), axis=0)
                    pm = cm if pm is None else jnp.maximum(pm, cm)
                if do_sm:
                    pts.append(jnp.exp2(st_scr[1 - slot][g, rows, :] - m_new).astype(BF16))
                    if (c + 1) * ATT_CHUNK % MXU_DIM == 0:
                        kt = (c + 1) * ATT_CHUNK // MXU_DIM - 1
                        cols = slice(kt * MXU_DIM, (kt + 1) * MXU_DIM)
                        acc = acc + _dot(vt_ref[t - 1, :, cols], jnp.concatenate(pts, axis=0))
                        pts = []
            if do_qk:
                n_pmax[g] = pm
            if do_sm:
                n_accs[g] = acc
        return tuple(n_pmax), tuple(n_ms), tuple(n_accs)

    neg = tuple(jnp.full((1, tq), -jnp.inf, F32) for _ in range(GROUP_A))
    neg8 = tuple(jnp.full((8, tq), -jnp.inf, F32) for _ in range(GROUP_A))
    state = (neg8, neg, tuple(jnp.zeros((VT_ROWS, tq), F32) for _ in range(GROUP_A)))
    state = tick(0, 0, True, False, state)
    state = tick(1, 1, True, True, state)
    ntrip = (nk - 2) // ATT_TICKS

    def body(i, state):
        for u in range(ATT_TICKS):
            state = tick(ATT_TICKS * i + 2 + u, u % 2, True, True, state)
        return state

    state = lax.fori_loop(0, ntrip, body, state)
    for t in range(2 + ntrip * ATT_TICKS, nk):
        state = tick(t, t % 2, True, True, state)
    state = tick(nk, nk % 2, False, True, state)
    ot = jnp.concatenate([acc[:DH_A] * (1.0 / acc[DH_A:DH_A + 1]) for acc in state[2]], axis=0)
    o_ref[...] = ot.T.astype(BF16)


def _attn(qt, kc, vtc, tq):
    bsz, _, _, seq = qt.shape
    nk, tk = kc.shape[2], kc.shape[3]
    assert nk >= 2 and tk % MXU_DIM == 0
    return pl.pallas_call(
        functools.partial(_attn_kernel, nk=nk),
        grid=(bsz, N_KV_A, seq // tq),
        in_specs=[pl.BlockSpec((None, GROUP_A, DH_A, tq), lambda b, h, i: (b, h, 0, i)),
                  pl.BlockSpec((None, None, nk, tk, DH_A), lambda b, h, i: (b, h, 0, 0, 0)),
                  pl.BlockSpec((None, None, nk, VT_ROWS, tk), lambda b, h, i: (b, h, 0, 0, 0))],
        out_specs=pl.BlockSpec((None, tq, GROUP_A * DH_A), lambda b, h, i: (b, i, h)),
        out_shape=jax.ShapeDtypeStruct((bsz, seq, W_A), BF16),
        scratch_shapes=[pltpu.VMEM((GROUP_A, tk, tq), F32), pltpu.VMEM((GROUP_A, tk, tq), F32)],
        compiler_params=_cparams(("parallel", "parallel", "arbitrary")),
        name="attn",
    )(qt, kc, vtc)


def _final_kernel(x_ref, hf_ref, hb_ref, gm_ref, att_ref, gpre_ref, gfin_ref,
                  wz_ref, wmo_ref, wao_ref, wout_ref, y_ref):
    x = x_ref[...]
    h = _rms_bf16(x, gpre_ref[...])
    z_a = _dot(h, wz_ref[:, :W_A])
    um = ((hf_ref[...].astype(F32) + hb_ref[...].astype(F32)) * gm_ref[...].astype(F32)).astype(BF16)
    ua = (att_ref[...].astype(F32) * (z_a * jax.nn.sigmoid(z_a))).astype(BF16)
    y_m = _dot(um, wmo_ref[...])
    y_a = _dot(ua, wao_ref[...])
    g_m = _dot(h, wz_ref[:, W_A:W_A + D_MODEL])
    g_a = _dot(h, wz_ref[:, W_A + D_MODEL:])
    merged = jax.nn.sigmoid(g_m) * y_m + jax.nn.sigmoid(g_a) * y_a
    out = x + _dot(merged.astype(BF16), wout_ref[...])
    ms = jnp.mean(out * out, axis=-1, keepdims=True)
    y_ref[...] = out * lax.rsqrt(ms + EPS) * gfin_ref[...]


def _final(x2, hf, hb, gm, att, gpre, gfin, wz, wmo, wao, wout, tm):
    t = x2.shape[0]
    tok = lambda i: (i, 0)
    fixed = lambda i: (0, 0)
    act = pl.BlockSpec((tm, D_MODEL), tok)
    sq = pl.BlockSpec((D_MODEL, D_MODEL), fixed)
    vec = pl.BlockSpec((1, D_MODEL), fixed)
    return pl.pallas_call(
        _final_kernel,
        grid=(t // tm,),
        in_specs=[act, act, act, act, act, vec, vec, pl.BlockSpec((D_MODEL, 3 * D_MODEL), fixed), sq, sq, sq],
        out_specs=act,
        out_shape=jax.ShapeDtypeStruct((t, D_MODEL), F32),
        compiler_params=_cparams(("parallel",)),
        name="final",
    )(x2, hf, hb, gm, att, gpre, gfin, wz, wmo, wao, wout)


def _rope_tables(seq, q_gain, k_gain):
    half = AXIS_DIM // 2
    d = np.arange(DH_A)
    axis = d // AXIS_DIM
    lo = (d % AXIS_DIM) < half
    idx = (d % AXIS_DIM) % half
    partner = np.where(lo, d + half, d - half)
    freqs = ROPE_THETA ** (-jnp.arange(0, AXIS_DIM, 2, dtype=F32) / AXIS_DIM)
    s = jnp.arange(seq)
    pos = jnp.where(jnp.asarray(axis)[:, None] == 0, (s // GRID_W)[None, :], (s % GRID_W)[None, :])
    ang = pos.astype(F32) * freqs[jnp.asarray(idx)][:, None]
    cos, sin = jnp.cos(ang), jnp.sin(ang)
    sign = jnp.where(jnp.asarray(lo), -1.0, 1.0)[:, None]

    def tables(gain, scale):
        g = gain.astype(F32) * scale
        return g[:, None] * cos, sign * g[jnp.asarray(partner)][:, None] * sin

    return tables(q_gain, LOG2E * DH_A ** -0.5) + tables(k_gain, 1.0)


def _prep_weights(w_in, b_if, norm_pre, w_mo, w_ao, w_out, norm_final):
    c0 = 5 * W_M
    c1 = c0 + 4 * N_HEADS_M
    c2 = c1 + W_A
    c3 = c2 + W_KV
    c4 = c3 + W_KV
    wb = w_in.astype(BF16)
    return dict(
        w_m=jnp.concatenate([wb[:, :W_M], wb[:, 2 * W_M:c0]], axis=1),
        w_kt=wb[:, W_M:2 * W_M].T,
        w_if=jnp.pad(wb[:, c0:c1], ((0, 0), (0, LANES - 4 * N_HEADS_M))),
        w_qt=wb[:, c1:c2].T,
        w_kta=wb[:, c2:c3].T,
        w_vt=wb[:, c3:c4].T,
        w_z=wb[:, c4:],
        bias=jnp.transpose(b_if.astype(F32), (2, 0, 1)).reshape(N_HEADS_M, 4),
        g_pre=norm_pre.astype(F32).reshape(1, D_MODEL),
        g_fin=norm_final.astype(F32).reshape(1, D_MODEL),
        w_mo=w_mo.astype(BF16), w_ao=w_ao.astype(BF16), w_out=w_out.astype(BF16),
    )


def _tiles(seq):
    tk = min(2 * MXU_DIM, seq)
    tq = tk if seq // tk >= 2 * ATT_TICKS + 2 else tk // 2
    return min(512, seq), min(4096, seq), tq, tk, min(MXU_DIM, seq)


def _layer(x, p, q_gain, k_gain):
    bsz, seq, _ = x.shape
    tm, sb, tq, tk, tf = _tiles(seq)
    x2 = x.reshape(bsz * seq, D_MODEL)

    q_m, kt, v_m, gate_m, if_raw = _mproj(x2, p["g_pre"], p["w_m"], p["w_kt"], p["w_if"], tm)
    qt, kc, vtc = _aproj(x, p["g_pre"], p["w_qt"], p["w_kta"], p["w_vt"], *_rope_tables(seq, q_gain, k_gain), tk)

    shp = (bsz, seq, W_M)
    gates = if_raw[:, :4 * N_HEADS_M].reshape(bsz, seq // PAIR, PAIR, 4, N_HEADS_M)
    gates = jnp.transpose(gates, (0, 4, 3, 1, 2))
    h_f, h_b = _mlstm(p["bias"], q_m.reshape(shp), v_m.reshape(shp), kt, gates, sb)

    att = _attn(qt, kc, vtc, tq)

    y = _final(x2, h_f.reshape(-1, W_M), h_b.reshape(-1, W_M), gate_m, att.reshape(-1, W_A),
               p["g_pre"], p["g_fin"], p["w_z"], p["w_mo"], p["w_ao"], p["w_out"], tf)
    return y.reshape(bsz, seq, D_MODEL)


def kernel(x_prompt, x_sample, w_in, b_if, norm_pre, q_gain, k_gain, w_mo, w_ao, w_out, norm_final):
    assert w_in.shape[0] == 1, "single-layer model"
    p = _prep_weights(w_in[0], b_if[0], norm_pre[0], w_mo[0], w_ao[0], w_out[0], norm_final)
    return (_layer(x_prompt, p, q_gain[0], k_gain[0]), _layer(x_sample, p, q_gain[0], k_gain[0]))
```

```python
import functools

import numpy as np
import jax
import jax.numpy as jnp
from jax import lax
from jax.experimental import pallas as pl
from jax.experimental.pallas import tpu as pltpu

F32 = jnp.float32
BF16 = jnp.bfloat16

D_MODEL = 1024
N_HEADS_M = 4
DH_M = 256
W_M = N_HEADS_M * DH_M
CHUNK = 64
N_HEADS_A = 16
N_KV_A = 4
GROUP_A = N_HEADS_A // N_KV_A
DH_A = 64
W_A = N_HEADS_A * DH_A
W_KV = N_KV_A * DH_A
GRID_W = 64
AXIS_DIM = DH_A // 2
ROPE_THETA = 10000.0
EPS = 1e-6

LANES = 128
VT_ROWS = 80
MXU_DIM = 256
ATT_CHUNK = 128
ATT_TICKS = 6
LOG2E = 1.4426950408889634
PAIR = 2 * CHUNK
MLSTM_UNROLL = 2
VMEM_LIMIT = 56 * 1024 * 1024


def _cparams(sem):
    return pltpu.CompilerParams(dimension_semantics=sem, vmem_limit_bytes=VMEM_LIMIT)


def _rms_bf16(x, g):
    ms = jnp.mean(x * x, axis=-1, keepdims=True)
    return (x * lax.rsqrt(ms + EPS) * g).astype(BF16)


def _dot(a, b):
    return jnp.dot(a, b, preferred_element_type=F32)


def _dot_nt(a, b):
    return lax.dot_general(a, b, (((1,), (1,)), ((), ())), preferred_element_type=F32)


def _mproj_body(h, tm, w_ref, wkt_ref, wif_ref, q_ref, kt_ref, v_ref, gate_ref, if_ref):
    def proj(c):
        return _dot(h, w_ref[:, c * W_M:(c + 1) * W_M])

    q_ref[...] = proj(0).astype(BF16)
    v_ref[...] = proj(1).astype(BF16)
    o = proj(2)
    z = proj(3)
    gate_ref[...] = (jax.nn.sigmoid(o) * (z * jax.nn.sigmoid(z))).astype(BF16)
    if_ref[...] = _dot(h, wif_ref[...])
    kt = _dot_nt(wkt_ref[...], h) * (DH_M ** -0.5)
    for hh in range(N_HEADS_M):
        for p in range(tm // PAIR):
            kt_ref[hh, p] = kt[hh * DH_M:(hh + 1) * DH_M, p * PAIR:(p + 1) * PAIR].astype(BF16)


def _aproj_body(h, tm, wqt_ref, wkt_ref, wvt_ref, qa_ref, qb_ref, ka_ref, kb_ref, qt_ref, k_ref, vt_ref):
    half = AXIS_DIM // 2

    def norm_rope(w_ref, a_ref, b_ref, nh):
        t = _dot_nt(w_ref[...], h).reshape(nh, DH_A, tm)
        xh = t * lax.rsqrt(jnp.mean(t * t, axis=1, keepdims=True) + EPS)
        x5 = xh.reshape(nh, 2, 2, half, tm)
        partner = jnp.concatenate([x5[:, :, 1:2], x5[:, :, 0:1]], axis=2).reshape(nh, DH_A, tm)
        return xh * a_ref[...] + partner * b_ref[...]

    qt_ref[...] = norm_rope(wqt_ref, qa_ref, qb_ref, N_HEADS_A).astype(BF16)

    ko = norm_rope(wkt_ref, ka_ref, kb_ref, N_KV_A)
    kpad = jnp.concatenate([ko, jnp.zeros_like(ko)], axis=1).reshape(N_KV_A * LANES, tm)
    knat = kpad.T
    for hh in range(N_KV_A):
        k_ref[hh] = knat[:, hh * LANES:hh * LANES + DH_A].astype(BF16)

    vt = _dot_nt(wvt_ref[...], h)
    row = lax.broadcasted_iota(jnp.int32, (VT_ROWS - DH_A, tm), 0)
    extra = jnp.where(row == 0, 1.0, 0.0).astype(BF16)
    for hh in range(N_KV_A):
        vt_ref[hh, :DH_A, :] = vt[hh * DH_A:(hh + 1) * DH_A, :].astype(BF16)
        vt_ref[hh, DH_A:, :] = extra


def _proj_kernel(x_ref, g_ref, w_ref, wkt_ref, wif_ref, wqt_ref, wkta_ref, wvt_ref, qa_ref, qb_ref, ka_ref, kb_ref,
                 q_ref, kt_ref, v_ref, gate_ref, if_ref, qt_ref, k_ref, vt_ref):
    tm = x_ref.shape[0]
    h = _rms_bf16(x_ref[...], g_ref[...])
    _mproj_body(h, tm, w_ref, wkt_ref, wif_ref, q_ref, kt_ref, v_ref, gate_ref, if_ref)
    _aproj_body(h, tm, wqt_ref, wkta_ref, wvt_ref, qa_ref, qb_ref, ka_ref, kb_ref, qt_ref, k_ref, vt_ref)


def _proj(x3, g, w, wkt, wif, wqt, wkta, wvt, qa, qb, ka, kb, tk):
    bsz, seq, _ = x3.shape
    nk = seq // tk
    fixed = lambda b, j: (0, 0)
    tok = lambda b, j: (b, j, 0)
    tab = pl.BlockSpec((DH_A, tk), lambda b, j: (0, j))
    big = jax.ShapeDtypeStruct((bsz, seq, W_M), BF16)
    act = pl.BlockSpec((None, tk, W_M), tok)
    return pl.pallas_call(
        _proj_kernel,
        grid=(bsz, nk),
        in_specs=[pl.BlockSpec((None, tk, D_MODEL), tok), pl.BlockSpec((1, D_MODEL), fixed),
                  pl.BlockSpec((D_MODEL, 4 * W_M), fixed), pl.BlockSpec((W_M, D_MODEL), fixed),
                  pl.BlockSpec((D_MODEL, LANES), fixed),
                  pl.BlockSpec((W_A, D_MODEL), fixed), pl.BlockSpec((W_KV, D_MODEL), fixed),
                  pl.BlockSpec((W_KV, D_MODEL), fixed), tab, tab, tab, tab],
        out_specs=[act,
                   pl.BlockSpec((N_HEADS_M, tk // PAIR, DH_M, PAIR), lambda b, j: (0, b * nk + j, 0, 0)),
                   act, act, pl.BlockSpec((None, tk, LANES), tok),
                   pl.BlockSpec((None, N_HEADS_A, DH_A, tk), lambda b, j: (b, 0, 0, j)),
                   pl.BlockSpec((None, N_KV_A, None, tk, DH_A), lambda b, j: (b, 0, j, 0, 0)),
                   pl.BlockSpec((None, N_KV_A, None, VT_ROWS, tk), lambda b, j: (b, 0, j, 0, 0))],
        out_shape=[big, jax.ShapeDtypeStruct((N_HEADS_M, bsz * seq // PAIR, DH_M, PAIR), BF16), big, big,
                   jax.ShapeDtypeStruct((bsz, seq, LANES), F32),
                   jax.ShapeDtypeStruct((bsz, N_HEADS_A, DH_A, seq), BF16),
                   jax.ShapeDtypeStruct((bsz, N_KV_A, nk, tk, DH_A), BF16),
                   jax.ShapeDtypeStruct((bsz, N_KV_A, nk, VT_ROWS, tk), BF16)],
        compiler_params=_cparams(("parallel", "parallel")),
        name="proj",
    )(x3, g, w, wkt, wif, wqt, wkta, wvt, qa, qb, ka, kb)


def _split3(x):
    h1 = x.astype(BF16)
    r1 = x - h1.astype(F32)
    h2 = r1.astype(BF16)
    return h1, h2, (r1 - h2.astype(F32)).astype(BF16)


def _mlstm_kernel(bias_ref,
                  qf_ref, vf_ref, ktf_ref, gf_ref,
                  qb_ref, vb_ref, ktb_ref, gb_ref,
                  hf_ref, hb_ref,
                  cta_scr, m_scr, a_scr, lf_scr, mc_scr, *, npb):
    head = pl.program_id(1)

    @pl.when(pl.program_id(2) == 0)
    def _():
        cta_scr[...] = jnp.zeros_like(cta_scr)
        m_scr[...] = jnp.zeros_like(m_scr)

    ti = lax.broadcasted_iota(jnp.int32, (CHUNK, PAIR), 0)
    li = lax.broadcasted_iota(jnp.int32, (CHUNK, PAIR), 1)
    ri = lax.broadcasted_iota(jnp.int32, (PAIR, PAIR), 0)
    ci = lax.broadcasted_iota(jnp.int32, (PAIR, PAIR), 1)
    same_chunk = (ri >= CHUNK) == (ci >= CHUNK)
    low_half = lax.broadcasted_iota(jnp.int32, (npb, PAIR), 1) < CHUNK
    ones_b = jnp.ones((PAIR, LANES), BF16)

    def valid_mask(d, par):
        rel = li - CHUNK * par
        return (rel >= ti) & (rel < CHUNK) if d else (rel <= ti) & (rel >= 0)

    for d, g_ref in ((0, gf_ref), (1, gb_ref)):
        last = 0 if d else CHUNK - 1
        xf = g_ref[2 * d + 1] + bias_ref[head, 2 * d + 1]
        lf = (jnp.minimum(xf, 0.0) - jnp.log(1.0 + jnp.exp(-jnp.abs(xf)))) * LOG2E
        cmat = jnp.where(same_chunk & ((ri >= ci) if d else (ri <= ci)), 1.0, 0.0).astype(BF16)
        b = sum(_dot(t, cmat) for t in _split3(lf))
        a = (g_ref[2 * d] + bias_ref[head, 2 * d]) * LOG2E - b
        a_scr[d] = a
        lf_scr[d] = lf
        amax = [jnp.broadcast_to(jnp.max(jnp.where(low_half == (par == 0), a, -jnp.inf), axis=1, keepdims=True),
                                 (npb, LANES)) for par in (0, 1)]
        blast = [jnp.broadcast_to(b[:, CHUNK * par + last:CHUNK * par + last + 1], (npb, LANES)) for par in (0, 1)]
        m = m_scr[d:d + 1, :]
        for p in (range(npb - 1, -1, -1) if d else range(npb)):
            for par in ((1, 0) if d else (0, 1)):
                c = 2 * p + par
                mc_scr[d, c:c + 1, :] = m
                m = blast[par][p:p + 1, :] + jnp.maximum(m, amax[par][p:p + 1, :])
        m_scr[d:d + 1, :] = m

    dirs = ((qf_ref, vf_ref, ktf_ref, hf_ref), (qb_ref, vb_ref, ktb_ref, hb_ref))

    def rows_of(c):
        return pl.ds(pl.multiple_of(c * CHUNK, CHUNK), CHUNK)

    def stage1(d, p, par):
        q_ref, _, kt_ref, _ = dirs[d]
        c = 2 * p + par
        valid = valid_mask(d, par)
        last = 0 if d else CHUNK - 1
        a_row = a_scr[d, pl.ds(p, 1), :]
        lf_row = lf_scr[d, pl.ds(p, 1), :]
        m = mc_scr[d, pl.ds(c, 1), 0:1]
        cm = jnp.max(jnp.where(valid, a_row, -jnp.inf), axis=1, keepdims=True)
        bcol = jnp.sum(jnp.where(valid, lf_row, 0.0), axis=1, keepdims=True)
        mc = jnp.maximum(cm, m)
        dmat = jnp.where(valid, jnp.exp2(a_row - mc), 0.0)
        inter = jnp.broadcast_to(jnp.exp2(m - mc), (CHUNK, LANES))
        emt = jnp.broadcast_to(jnp.exp2(-(bcol + mc)), (CHUNK, LANES))
        dec = jnp.broadcast_to(jnp.exp2(m - mc[last:last + 1]), (1, LANES))
        sd = (_dot(q_ref[rows_of(c), :], kt_ref[p]) * dmat).astype(BF16)
        return sd, dmat[last:last + 1, :], inter, emt, dec

    def stage2(d, p, par, pre):
        sd, ws_row, inter, emt, dec = pre
        q_ref, v_ref, kt_ref, h_ref = dirs[d]
        rows = rows_of(2 * p + par)
        q = q_ref[rows, :]
        v_pair = v_ref[pl.ds(pl.multiple_of(p * PAIR, PAIR), PAIR), :]
        v_aug = jnp.concatenate([v_pair, ones_b], axis=1)
        cta = cta_scr[d]
        ktw = (kt_ref[p].astype(F32) * ws_row).astype(BF16)
        both = _dot(jnp.concatenate([sd, ktw], axis=0), v_aug)
        na = both[:CHUNK] + jnp.concatenate([inter, inter, inter], axis=1) * _dot(q, cta.astype(BF16))
        r = 1.0 / jnp.maximum(jnp.abs(na[:, DH_M:]), emt)
        h_ref[rows, :] = (na[:, :DH_M] * jnp.concatenate([r, r], axis=1)).astype(BF16)
        cta_scr[d] = jnp.concatenate([dec, dec, dec], axis=1) * cta + both[CHUNK:]

    def body(i, carry):
        pf = i
        pb = npb - 1 - i
        mid_f = stage1(0, pf, 1)
        mid_b = stage1(1, pb, 0)
        stage2(0, pf, 0, carry[0])
        stage2(1, pb, 1, carry[1])
        nxt_f = stage1(0, jnp.minimum(pf + 1, npb - 1), 0)
        nxt_b = stage1(1, jnp.maximum(pb - 1, 0), 1)
        stage2(0, pf, 1, mid_f)
        stage2(1, pb, 0, mid_b)
        return nxt_f, nxt_b

    lax.fori_loop(0, npb, body, (stage1(0, 0, 0), stage1(1, npb - 1, 1)), unroll=MLSTM_UNROLL)


def _mlstm(bias, q, v, kt, gates, sb):
    bsz, seq, _ = q.shape
    nb = seq // sb
    npb = sb // PAIR
    fwd = lambda b, h, j: (b, j, h)
    bwd = lambda b, h, j: (b, nb - 1 - j, h)
    kfwd = lambda b, h, j: (h, b * nb + j, 0, 0)
    kbwd = lambda b, h, j: (h, b * nb + nb - 1 - j, 0, 0)
    gfwd = lambda b, h, j: (b, h, 0, j, 0)
    gbwd = lambda b, h, j: (b, h, 0, nb - 1 - j, 0)
    tok = lambda im: pl.BlockSpec((None, sb, DH_M), im)
    ktb = lambda im: pl.BlockSpec((None, npb, DH_M, PAIR), im)
    gb = lambda im: pl.BlockSpec((None, None, 4, npb, PAIR), im)
    out = jax.ShapeDtypeStruct((bsz, seq, W_M), BF16)
    return pl.pallas_call(
        functools.partial(_mlstm_kernel, npb=npb),
        grid=(bsz, N_HEADS_M, nb),
        in_specs=[pl.BlockSpec(memory_space=pltpu.SMEM),
                  tok(fwd), tok(fwd), ktb(kfwd), gb(gfwd),
                  tok(bwd), tok(bwd), ktb(kbwd), gb(gbwd)],
        out_specs=[tok(fwd), tok(bwd)],
        out_shape=[out, out],
        scratch_shapes=[pltpu.VMEM((2, DH_M, DH_M + LANES), F32), pltpu.VMEM((8, LANES), F32),
                        pltpu.VMEM((2, npb, PAIR), F32), pltpu.VMEM((2, npb, PAIR), F32),
                        pltpu.VMEM((2, 2 * npb, LANES), F32)],
        compiler_params=_cparams(("parallel", "parallel", "arbitrary")),
        name="mlstm",
    )(bias, q, v, kt, gates, q, v, kt, gates)


def _attn_kernel(qt_ref, k_ref, vt_ref, o_ref, st0, st1, *, nk):
    st_scr = (st0, st1)
    tq = qt_ref.shape[-1]
    tk = k_ref.shape[1]
    qs = [qt_ref[g] for g in range(GROUP_A)]

    def tick(t, slot, do_qk, do_sm, state):
        pmax, ms, accs = state
        n_pmax, n_ms, n_accs = list(pmax), list(ms), list(accs)
        for g in range(GROUP_A):
            if do_sm:
                m_new = jnp.maximum(ms[g], jnp.max(pmax[g], axis=0, keepdims=True))
                acc = jnp.exp2(ms[g] - m_new) * accs[g]
                n_ms[g] = m_new
            pm = None
            pts = []
            for c in range(tk // ATT_CHUNK):
                rows = slice(c * ATT_CHUNK, (c + 1) * ATT_CHUNK)
                if do_qk:
                    st = _dot(k_ref[t, rows, :], qs[g])
                    st_scr[slot][g, rows, :] = st
                    cm = jnp.max(st.reshape(ATT_CHUNK // 8, 8, tq), axis=0)
                    pm = cm if pm is None else jnp.maximum(pm, cm)
                if do_sm:
                    pts.append(jnp.exp2(st_scr[1 - slot][g, rows, :] - m_new).astype(BF16))
                    if (c + 1) * ATT_CHUNK % MXU_DIM == 0:
                        kt = (c + 1) * ATT_CHUNK // MXU_DIM - 1
                        cols = slice(kt * MXU_DIM, (kt + 1) * MXU_DIM)
                        acc = acc + _dot(vt_ref[t - 1, :, cols], jnp.concatenate(pts, axis=0))
                        pts = []
            if do_qk:
                n_pmax[g] = pm
            if do_sm:
                n_accs[g] = acc
        return tuple(n_pmax), tuple(n_ms), tuple(n_accs)

    neg = tuple(jnp.full((1, tq), -jnp.inf, F32) for _ in range(GROUP_A))
    neg8 = tuple(jnp.full((8, tq), -jnp.inf, F32) for _ in range(GROUP_A))
    state = (neg8, neg, tuple(jnp.zeros((VT_ROWS, tq), F32) for _ in range(GROUP_A)))
    state = tick(0, 0, True, False, state)
    state = tick(1, 1, True, True, state)
    ntrip = (nk - 2) // ATT_TICKS

    def body(i, state):
        for u in range(ATT_TICKS):
            state = tick(ATT_TICKS * i + 2 + u, u % 2, True, True, state)
        return state

    state = lax.fori_loop(0, ntrip, body, state)
    for t in range(2 + ntrip * ATT_TICKS, nk):
        state = tick(t, t % 2, True, True, state)
    state = tick(nk, nk % 2, False, True, state)
    ot = jnp.concatenate([acc[:DH_A] * (1.0 / acc[DH_A:DH_A + 1]) for acc in state[2]], axis=0)
    o_ref[...] = ot.T.astype(BF16)


def _attn(qt, kc, vtc, tq):
    bsz, _, _, seq = qt.shape
    nk, tk = kc.shape[2], kc.shape[3]
    assert nk >= 2 and tk % MXU_DIM == 0
    return pl.pallas_call(
        functools.partial(_attn_kernel, nk=nk),
        grid=(bsz, N_KV_A, seq // tq),
        in_specs=[pl.BlockSpec((None, GROUP_A, DH_A, tq), lambda b, h, i: (b, h, 0, i)),
                  pl.BlockSpec((None, None, nk, tk, DH_A), lambda b, h, i: (b, h, 0, 0, 0)),
                  pl.BlockSpec((None, None, nk, VT_ROWS, tk), lambda b, h, i: (b, h, 0, 0, 0))],
        out_specs=pl.BlockSpec((None, tq, GROUP_A * DH_A), lambda b, h, i: (b, i, h)),
        out_shape=jax.ShapeDtypeStruct((bsz, seq, W_A), BF16),
        scratch_shapes=[pltpu.VMEM((GROUP_A, tk, tq), F32), pltpu.VMEM((GROUP_A, tk, tq), F32)],
        compiler_params=_cparams(("parallel", "parallel", "arbitrary")),
        name="attn",
    )(qt, kc, vtc)


def _final_kernel(x_ref, hf_ref, hb_ref, gm_ref, att_ref, gpre_ref, gfin_ref,
                  wz_ref, wmo_ref, wao_ref, wout_ref, y_ref):
    x = x_ref[...]
    h = _rms_bf16(x, gpre_ref[...])
    z_a = _dot(h, wz_ref[:, :W_A])
    um = ((hf_ref[...].astype(F32) + hb_ref[...].astype(F32)) * gm_ref[...].astype(F32)).astype(BF16)
    ua = (att_ref[...].astype(F32) * (z_a * jax.nn.sigmoid(z_a))).astype(BF16)
    y_m = _dot(um, wmo_ref[...])
    y_a = _dot(ua, wao_ref[...])
    g_m = _dot(h, wz_ref[:, W_A:W_A + D_MODEL])
    g_a = _dot(h, wz_ref[:, W_A + D_MODEL:])
    merged = jax.nn.sigmoid(g_m) * y_m + jax.nn.sigmoid(g_a) * y_a
    out = x + _dot(merged.astype(BF16), wout_ref[...])
    ms = jnp.mean(out * out, axis=-1, keepdims=True)
    y_ref[...] = out * lax.rsqrt(ms + EPS) * gfin_ref[...]


def _final(x2, hf, hb, gm, att, gpre, gfin, wz, wmo, wao, wout, tm):
    t = x2.shape[0]
    tok = lambda i: (i, 0)
    fixed = lambda i: (0, 0)
    act = pl.BlockSpec((tm, D_MODEL), tok)
    sq = pl.BlockSpec((D_MODEL, D_MODEL), fixed)
    vec = pl.BlockSpec((1, D_MODEL), fixed)
    return pl.pallas_call(
        _final_kernel,
        grid=(t // tm,),
        in_specs=[act, act, act, act, act, vec, vec, pl.BlockSpec((D_MODEL, 3 * D_MODEL), fixed), sq, sq, sq],
        out_specs=act,
        out_shape=jax.ShapeDtypeStruct((t, D_MODEL), F32),
        compiler_params=_cparams(("parallel",)),
        name="final",
    )(x2, hf, hb, gm, att, gpre, gfin, wz, wmo, wao, wout)


def _rope_tables(seq, q_gain, k_gain):
    half = AXIS_DIM // 2
    d = np.arange(DH_A)
    axis = d // AXIS_DIM
    lo = (d % AXIS_DIM) < half
    idx = (d % AXIS_DIM) % half
    partner = np.where(lo, d + half, d - half)
    freqs = ROPE_THETA ** (-jnp.arange(0, AXIS_DIM, 2, dtype=F32) / AXIS_DIM)
    s = jnp.arange(seq)
    pos = jnp.where(jnp.asarray(axis)[:, None] == 0, (s // GRID_W)[None, :], (s % GRID_W)[None, :])
    ang = pos.astype(F32) * freqs[jnp.asarray(idx)][:, None]
    cos, sin = jnp.cos(ang), jnp.sin(ang)
    sign = jnp.where(jnp.asarray(lo), -1.0, 1.0)[:, None]

    def tables(gain, scale):
        g = gain.astype(F32) * scale
        return g[:, None] * cos, sign * g[jnp.asarray(partner)][:, None] * sin

    return tables(q_gain, LOG2E * DH_A ** -0.5) + tables(k_gain, 1.0)


def _prep_weights(w_in, b_if, norm_pre, w_mo, w_ao, w_out, norm_final):
    c0 = 5 * W_M
    c1 = c0 + 4 * N_HEADS_M
    c2 = c1 + W_A
    c3 = c2 + W_KV
    c4 = c3 + W_KV
    wb = w_in.astype(BF16)
    return dict(
        w_m=jnp.concatenate([wb[:, :W_M], wb[:, 2 * W_M:c0]], axis=1),
        w_kt=wb[:, W_M:2 * W_M].T,
        w_if=jnp.pad(wb[:, c0:c1], ((0, 0), (0, LANES - 4 * N_HEADS_M))),
        w_qt=wb[:, c1:c2].T,
        w_kta=wb[:, c2:c3].T,
        w_vt=wb[:, c3:c4].T,
        w_z=wb[:, c4:],
        bias=jnp.transpose(b_if.astype(F32), (2, 0, 1)).reshape(N_HEADS_M, 4),
        g_pre=norm_pre.astype(F32).reshape(1, D_MODEL),
        g_fin=norm_final.astype(F32).reshape(1, D_MODEL),
        w_mo=w_mo.astype(BF16), w_ao=w_ao.astype(BF16), w_out=w_out.astype(BF16),
    )


def _tiles(seq):
    tk = min(2 * MXU_DIM, seq)
    tq = tk if seq // tk >= 2 * ATT_TICKS + 2 else tk // 2
    return min(512, seq), min(4096, seq), tq, tk, min(MXU_DIM, seq)


def _layer(x, p, q_gain, k_gain):
    bsz, seq, _ = x.shape
    tm, sb, tq, tk, tf = _tiles(seq)
    x2 = x.reshape(bsz * seq, D_MODEL)

    q_m, kt, v_m, gate_m, if_raw, qt, kc, vtc = _proj(
        x, p["g_pre"], p["w_m"], p["w_kt"], p["w_if"], p["w_qt"], p["w_kta"], p["w_vt"],
        *_rope_tables(seq, q_gain, k_gain), tk)

    gates = if_raw[:, :, :4 * N_HEADS_M].reshape(bsz, seq // PAIR, PAIR, 4, N_HEADS_M)
    gates = jnp.transpose(gates, (0, 4, 3, 1, 2))
    h_f, h_b = _mlstm(p["bias"], q_m, v_m, kt, gates, sb)

    att = _attn(qt, kc, vtc, tq)

    y = _final(x2, h_f.reshape(-1, W_M), h_b.reshape(-1, W_M), gate_m.reshape(-1, W_M), att.reshape(-1, W_A),
               p["g_pre"], p["g_fin"], p["w_z"], p["w_mo"], p["w_ao"], p["w_out"], tf)
    return y.reshape(bsz, seq, D_MODEL)


def kernel(x_prompt, x_sample, w_in, b_if, norm_pre, q_gain, k_gain, w_mo, w_ao, w_out, norm_final):
    assert w_in.shape[0] == 1, "single-layer model"
    p = _prep_weights(w_in[0], b_if[0], norm_pre[0], w_mo[0], w_ao[0], w_out[0], norm_final)
    return (_layer(x_prompt, p, q_gain[0], k_gain[0]), _layer(x_sample, p, q_gain[0], k_gain[0]))
```
